```python
import jax, jax.numpy as jnp
from jax import lax
import numpy as np

D_MODEL = 2048
BATCH = 4
SEQ = 4096
DEPTH = 2

HEAD_DIM = 128
MOBA_HEADS = 8
MOBA_WIDTH = MOBA_HEADS * HEAD_DIM
MOBA_BLOCK = 256
MOBA_TOPK = 3
MOBA_QCHUNK = 32
DN_HEADS = 8
DN_WIDTH = DN_HEADS * HEAD_DIM
DN_CONV = 4
DN_CHUNK = 64
D_FF = 5632
FFN_CONV = 3
NORM_EPS = 1e-6
IN_SIZES = (MOBA_WIDTH, MOBA_WIDTH, MOBA_WIDTH, 3 * DN_WIDTH, DN_WIDTH, DN_HEADS, DN_HEADS, D_MODEL, D_MODEL)
IN_COLS = sum(IN_SIZES)

kernel_name = "hybrid_moba_gdn_convglu"


def rms_norm(x, gain):
    xf = x.astype(jnp.float32)
    y = xf * lax.rsqrt(jnp.mean(xf * xf, axis=-1, keepdims=True) + NORM_EPS)
    return (y * gain.astype(jnp.float32)).astype(x.dtype)


def l2_norm(x):
    xf = x.astype(jnp.float32)
    return xf * lax.rsqrt(jnp.sum(xf * xf, axis=-1, keepdims=True) + NORM_EPS)


def causal_dwconv(x, w):
    k_width = w.shape[0]
    s = x.shape[1]
    xp = jnp.pad(x, ((0, 0), (k_width - 1, 0), (0, 0)))
    y = xp[:, 0:s] * w[0]
    for i in range(1, k_width):
        y = y + xp[:, i:i + s] * w[i]
    return y


def moba_branch(q, k, v, q_gain, k_gain):
    b, s, _ = q.shape
    shp = (b, s, MOBA_HEADS, HEAD_DIM)
    q = rms_norm(q.reshape(shp), q_gain)
    k = rms_norm(k.reshape(shp), k_gain)
    v = v.reshape(shp)
    sp = -(-s // MOBA_BLOCK) * MOBA_BLOCK
    pad = ((0, 0), (0, sp - s), (0, 0), (0, 0))
    q, k, v = [jnp.pad(t, pad).transpose(0, 2, 1, 3) for t in (q, k, v)]
    nb = sp // MOBA_BLOCK
    topk = min(MOBA_TOPK, nb)
    k_blocks = k.reshape(b, MOBA_HEADS, nb, MOBA_BLOCK, HEAD_DIM)
    v_blocks = v.reshape(b, MOBA_HEADS, nb, MOBA_BLOCK, HEAD_DIM)
    k_mean = jnp.mean(k_blocks.astype(jnp.float32), axis=3)
    scale = HEAD_DIM ** -0.5
    gather = jax.vmap(jax.vmap(lambda blocks, idx: blocks[idx]))
    block_ids = jnp.arange(nb)
    q_offs = jnp.arange(MOBA_QCHUNK)
    k_offs = jnp.arange(MOBA_BLOCK)

    def chunk(c):
        start = c * MOBA_QCHUNK
        qc = lax.dynamic_slice_in_dim(q, start, MOBA_QCHUNK, axis=2)
        qblk = start // MOBA_BLOCK
        gate = jnp.einsum('bhqd,bhnd->bhqn', qc.astype(jnp.float32), k_mean)
        gate = jnp.where(block_ids < qblk, gate, -jnp.inf)
        _, idx = lax.top_k(gate, topk)
        valid = idx < qblk
        k_sel = gather(k_blocks, idx)
        v_sel = gather(v_blocks, idx)
        s_sel = jnp.einsum('bhqd,bhqnkd->bhqnk', qc, k_sel).astype(jnp.float32) * scale
        s_sel = jnp.where(valid[..., None], s_sel, -jnp.inf).reshape(b, MOBA_HEADS, MOBA_QCHUNK, topk * MOBA_BLOCK)
        k_own = lax.dynamic_slice_in_dim(k, qblk * MOBA_BLOCK, MOBA_BLOCK, axis=2)
        v_own = lax.dynamic_slice_in_dim(v, qblk * MOBA_BLOCK, MOBA_BLOCK, axis=2)
        s_own = jnp.einsum('bhqd,bhkd->bhqk', qc, k_own).astype(jnp.float32) * scale
        causal = (qblk * MOBA_BLOCK + k_offs)[None, :] <= (start + q_offs)[:, None]
        s_own = jnp.where(causal, s_own, -jnp.inf)
        p = jax.nn.softmax(jnp.concatenate([s_sel, s_own], axis=-1), axis=-1)
        p_sel = p[..., :topk * MOBA_BLOCK].reshape(b, MOBA_HEADS, MOBA_QCHUNK, topk, MOBA_BLOCK).astype(v.dtype)
        p_own = p[..., topk * MOBA_BLOCK:].astype(v.dtype)
        return (jnp.einsum('bhqnk,bhqnkd->bhqd', p_sel, v_sel)
                + jnp.einsum('bhqk,bhkd->bhqd', p_own, v_own))

    out = lax.map(chunk, jnp.arange(sp // MOBA_QCHUNK))
    out = out.transpose(1, 0, 3, 2, 4).reshape(b, sp, MOBA_WIDTH)
    return out[:, :s]


def chunk_gated_delta_rule(q, k, v, g, beta):
    b, s, h, dk = q.shape
    dv = v.shape[-1]
    c = DN_CHUNK
    nc = s // c
    q, k, v = [t.transpose(0, 2, 1, 3).reshape(b, h, nc, c, -1) for t in (q, k, v)]
    g, beta = [t.transpose(0, 2, 1).reshape(b, h, nc, c) for t in (g, beta)]
    G = jnp.cumsum(g, axis=-1)
    tril = jnp.tril(jnp.ones((c, c), bool))
    strict = jnp.tril(jnp.ones((c, c), bool), -1)
    decay = jnp.exp(jnp.where(tril, G[..., :, None] - G[..., None, :], -jnp.inf))
    kb = k * beta[..., None]
    m = jnp.einsum('bhnck,bhnsk->bhncs', kb, k) * decay
    lmat = jnp.where(strict, m, 0.0) + jnp.eye(c, dtype=m.dtype)
    u = lax.linalg.triangular_solve(lmat, v * beta[..., None], left_side=True, lower=True, unit_diagonal=True)
    w = lax.linalg.triangular_solve(lmat, kb * jnp.exp(G)[..., None], left_side=True, lower=True, unit_diagonal=True)
    attn = jnp.einsum('bhnck,bhnsk->bhncs', q, k) * decay
    qg = q * jnp.exp(G)[..., None]
    kd = k * jnp.exp(G[..., -1:] - G)[..., None]
    glast = jnp.exp(G[..., -1])
    xs = tuple(jnp.moveaxis(t, 2, 0) for t in (u, w, attn, qg, kd, glast))

    def step(state, inp):
        u_c, w_c, a_c, qg_c, kd_c, gl_c = inp
        v_new = u_c - jnp.einsum('bhck,bhkv->bhcv', w_c, state)
        o_c = jnp.einsum('bhck,bhkv->bhcv', qg_c, state) + jnp.einsum('bhcs,bhsv->bhcv', a_c, v_new)
        state = state * gl_c[..., None, None] + jnp.einsum('bhck,bhcv->bhkv', kd_c, v_new)
        return state, o_c

    state0 = jnp.zeros((b, h, dk, dv), jnp.float32)
    _, o = lax.scan(step, state0, xs)
    return o.transpose(1, 0, 3, 2, 4).reshape(b, s, h, dv)


def deltanet_branch(qkv, z, b_in, a_in, conv_w, a_log, dt_bias, out_gain):
    b, s, _ = qkv.shape
    qkv = jax.nn.silu(causal_dwconv(qkv, conv_w))
    q, k, v = jnp.split(qkv, 3, axis=-1)
    shp = (b, s, DN_HEADS, HEAD_DIM)
    q = l2_norm(q.reshape(shp)) * (HEAD_DIM ** -0.5)
    k = l2_norm(k.reshape(shp))
    v = v.reshape(shp).astype(jnp.float32)
    beta = jax.nn.sigmoid(b_in.astype(jnp.float32))
    g = -jnp.exp(a_log.astype(jnp.float32)) * jax.nn.softplus(a_in.astype(jnp.float32) + dt_bias.astype(jnp.float32))
    o = chunk_gated_delta_rule(q, k, v, g, beta)
    o = rms_norm(o, out_gain) * jax.nn.silu(z.reshape(shp).astype(jnp.float32))
    return o.reshape(b, s, DN_WIDTH).astype(qkv.dtype)


def conv_glu(h, w_in, conv_w, conv_b, w_down):
    gate, up = jnp.split(h @ w_in, 2, axis=-1)
    gate = causal_dwconv(gate, conv_w) + conv_b
    return (jax.nn.silu(gate) * up) @ w_down


def setup_inputs(seed: int = 0) -> dict:
    key = jax.random.key(seed)
    ks = jax.random.split(key, 20)

    def nrm(k, shape, scale):
        return jax.random.normal(k, shape, jnp.float32) * scale

    dt = jnp.exp(jax.random.uniform(ks[8], (DEPTH, DN_HEADS), jnp.float32, np.log(1e-3), np.log(1e-1)))
    return {
        "x": nrm(ks[0], (BATCH, SEQ, D_MODEL), 1.0),
        "attn_norm": 1.0 + nrm(ks[1], (DEPTH, D_MODEL), 0.02),
        "w_in": nrm(ks[2], (DEPTH, D_MODEL, IN_COLS), D_MODEL ** -0.5),
        "moba_q_norm": 1.0 + nrm(ks[3], (DEPTH, HEAD_DIM), 0.02),
        "moba_k_norm": 1.0 + nrm(ks[4], (DEPTH, HEAD_DIM), 0.02),
        "dn_conv": nrm(ks[5], (DEPTH, DN_CONV, 3 * DN_WIDTH), DN_CONV ** -0.5),
        "dn_a_log": jnp.log(jax.random.uniform(ks[6], (DEPTH, DN_HEADS), jnp.float32, 1.0, 16.0)),
        "dn_dt_bias": dt + jnp.log(-jnp.expm1(-dt)),
        "dn_out_norm": 1.0 + nrm(ks[7], (DEPTH, HEAD_DIM), 0.02),
        "w_branch_a": nrm(ks[9], (DEPTH, MOBA_WIDTH, D_MODEL), MOBA_WIDTH ** -0.5),
        "w_branch_b": nrm(ks[10], (DEPTH, DN_WIDTH, D_MODEL), DN_WIDTH ** -0.5),
        "w_out": nrm(ks[11], (DEPTH, D_MODEL, D_MODEL), D_MODEL ** -0.5),
        "ffn_norm": 1.0 + nrm(ks[12], (DEPTH, D_MODEL), 0.02),
        "w_ffn_in": nrm(ks[13], (DEPTH, D_MODEL, 2 * D_FF), D_MODEL ** -0.5),
        "ffn_conv": nrm(ks[14], (DEPTH, FFN_CONV, D_FF), FFN_CONV ** -0.5),
        "ffn_conv_bias": nrm(ks[15], (DEPTH, D_FF), 0.01),
        "w_ffn_down": nrm(ks[16], (DEPTH, D_FF, D_MODEL), D_FF ** -0.5),
    }


def reference(x, attn_norm, w_in, moba_q_norm, moba_k_norm, dn_conv, dn_a_log, dn_dt_bias, dn_out_norm,
              w_branch_a, w_branch_b, w_out, ffn_norm, w_ffn_in, ffn_conv, ffn_conv_bias, w_ffn_down):
    splits = [int(v) for v in np.cumsum(IN_SIZES)[:-1]]
    for l in range(DEPTH):
        h = rms_norm(x, attn_norm[l])
        proj = h @ w_in[l]
        mq, mk, mv, dqkv, dz, db, da, ga, gb = jnp.split(proj, splits, axis=-1)
        y_a = moba_branch(mq, mk, mv, moba_q_norm[l], moba_k_norm[l])
        y_b = deltanet_branch(dqkv, dz, db, da, dn_conv[l], dn_a_log[l], dn_dt_bias[l], dn_out_norm[l])
        merged = jax.nn.sigmoid(ga) * (y_a @ w_branch_a[l]) + jax.nn.sigmoid(gb) * (y_b @ w_branch_b[l])
        x = x + merged @ w_out[l]
        h = rms_norm(x, ffn_norm[l])
        x = x + conv_glu(h, w_ffn_in[l], ffn_conv[l], ffn_conv_bias[l], w_ffn_down[l])
    return x
```

```python
import functools

import jax
import jax.numpy as jnp
from jax import lax
from jax.experimental import pallas as pl
from jax.experimental.pallas import tpu as pltpu

F32 = jnp.float32
BF16 = jnp.bfloat16

D_MODEL = 2048
HEAD_DIM = 128
MOBA_HEADS = 8
MOBA_WIDTH = MOBA_HEADS * HEAD_DIM
MOBA_BLOCK = 256
MOBA_TOPK = 3
DN_HEADS = 8
DN_WIDTH = DN_HEADS * HEAD_DIM
DN_CONV = 4
D_FF = 5632
FFN_CONV = 3
NORM_EPS = 1e-6

PROJ_COLS = 3 * MOBA_WIDTH + 3 * DN_WIDTH + DN_WIDTH + 2 * D_MODEL
SMALL_COLS = 128

LANES = 128
VMEM_LIMIT_BYTES = 56 * 1024 * 1024

MM_TM = 1024
MM_TN = 1024
NORM_ROWS = 128
DN_TILE = 128
FFN_TN = 512
NEG = -1e30


def _sigmoid(x):
    return 1.0 / (1.0 + jnp.exp(-x))


def _dot(a, b):
    return jnp.dot(a, b, preferred_element_type=F32)


def _dot_nt(a, b, precision=None):
    return lax.dot_general(a, b, (((1,), (1,)), ((), ())), precision=precision,
                           preferred_element_type=F32)


def _dot_tn(a, b):
    return lax.dot_general(a, b, (((0,), (0,)), ((), ())), preferred_element_type=F32)


def _params(*sem):
    return pltpu.CompilerParams(dimension_semantics=sem, vmem_limit_bytes=VMEM_LIMIT_BYTES)


def _rms_rows(x_ref, g_ref, h_ref):
    rows = x_ref.shape[0]

    def body(c, carry):
        r = pl.multiple_of(c * NORM_ROWS, NORM_ROWS)
        x = x_ref[pl.ds(r, NORM_ROWS), :]
        ms = jnp.mean(x * x, axis=-1, keepdims=True)
        h_ref[pl.ds(r, NORM_ROWS), :] = (x * lax.rsqrt(ms + NORM_EPS) * g_ref[...]).astype(h_ref.dtype)
        return carry

    lax.fori_loop(0, rows // NORM_ROWS, body, 0)


def _norm_proj_small_kernel(x_ref, g_ref, w_ref, ws_ref, o_ref, s_ref, h_ref):
    @pl.when(pl.program_id(1) == 0)
    def _():
        _rms_rows(x_ref, g_ref, h_ref)
        s_ref[...] = _dot(h_ref[...], ws_ref[...])

    o_ref[...] = _dot(h_ref[...], w_ref[...]).astype(o_ref.dtype)


def _norm_proj_kernel(x_ref, g_ref, w_ref, o_ref, h_ref):
    @pl.when(pl.program_id(1) == 0)
    def _():
        _rms_rows(x_ref, g_ref, h_ref)

    o_ref[...] = _dot(h_ref[...], w_ref[...]).astype(o_ref.dtype)


def _norm_proj(x2, gain, w, w_small=None):
    n, d = x2.shape
    cols = w.shape[1]
    tm = min(MM_TM, n)
    grid = (n // tm, cols // MM_TN)
    in_specs = [
        pl.BlockSpec((tm, d), lambda i, j: (i, 0)),
        pl.BlockSpec((1, d), lambda i, j: (0, 0)),
        pl.BlockSpec((d, MM_TN), lambda i, j: (0, j)),
    ]
    out_main = jax.ShapeDtypeStruct((n, cols), BF16)
    spec_main = pl.BlockSpec((tm, MM_TN), lambda i, j: (i, j))
    scratch = [pltpu.VMEM((tm, d), BF16)]
    if w_small is None:
        return pl.pallas_call(
            _norm_proj_kernel, grid=grid, in_specs=in_specs, out_specs=spec_main,
            out_shape=out_main, scratch_shapes=scratch,
            compiler_params=_params("arbitrary", "arbitrary"), name="norm_proj",
        )(x2, gain, w)
    in_specs.append(pl.BlockSpec((d, SMALL_COLS), lambda i, j: (0, 0)))
    return pl.pallas_call(
        _norm_proj_small_kernel, grid=grid, in_specs=in_specs,
        out_specs=[spec_main, pl.BlockSpec((tm, SMALL_COLS), lambda i, j: (i, 0))],
        out_shape=[out_main, jax.ShapeDtypeStruct((n, SMALL_COLS), F32)],
        scratch_shapes=scratch,
        compiler_params=_params("arbitrary", "arbitrary"), name="norm_proj_small",
    )(x2, gain, w, w_small)


def _moba_kernel(q_ref, k_ref, v_ref, qg_ref, kg_ref, o_ref, kn_ref, vt_ref, km_ref, sel_ref):
    qi = pl.program_id(2)
    nb = kn_ref.shape[0]
    blk = MOBA_BLOCK

    @pl.when(qi == 0)
    def _():
        for n in range(nb):
            kb = k_ref[n * blk:(n + 1) * blk, :].astype(F32)
            ms = jnp.mean(kb * kb, axis=-1, keepdims=True)
            kn = kb * lax.rsqrt(ms + NORM_EPS) * kg_ref[...]
            kn_ref[n] = kn.astype(BF16)
            km_ref[n:n + 1, :] = jnp.mean(kn, axis=0, keepdims=True)
            vt_ref[n] = v_ref[n * blk:(n + 1) * blk, :].astype(F32).T.astype(BF16)

    q = q_ref[...].astype(F32)
    qn = q * lax.rsqrt(jnp.mean(q * q, axis=-1, keepdims=True) + NORM_EPS) * qg_ref[...]

    gate = _dot_nt(km_ref[...], qn, precision=lax.Precision.HIGHEST)
    rows = lax.broadcasted_iota(jnp.int32, gate.shape, 0).astype(F32)
    gate = jnp.where(rows < qi.astype(F32), gate, NEG)
    sel = jnp.zeros(gate.shape, F32)
    for _ in range(MOBA_TOPK):
        mx = jnp.max(gate, axis=0, keepdims=True)
        idx = jnp.min(jnp.where(gate == mx, rows, float(nb)), axis=0, keepdims=True)
        pick = jnp.logical_and(rows == idx, mx > 0.5 * NEG)
        sel = jnp.where(pick, 1.0, sel)
        gate = jnp.where(pick, NEG, gate)
    sel_ref[...] = sel

    qb = (qn * (HEAD_DIM ** -0.5)).astype(BF16)

    s = _dot_nt(kn_ref[qi], qb)
    kidx = lax.broadcasted_iota(jnp.int32, s.shape, 0)
    qidx = lax.broadcasted_iota(jnp.int32, s.shape, 1)
    s = jnp.where(kidx <= qidx, s, NEG)
    m0 = jnp.max(s, axis=0, keepdims=True)
    p = jnp.exp(s - m0)
    l0 = jnp.sum(p, axis=0, keepdims=True)
    acc0 = _dot(vt_ref[qi], p.astype(BF16))

    def body(j, carry):
        m, l, acc = carry
        sj = _dot_nt(kn_ref[j], qb)
        sj = jnp.where(sel_ref[pl.ds(j, 1), :] > 0.0, sj, NEG)
        m_new = jnp.maximum(m, jnp.max(sj, axis=0, keepdims=True))
        alpha = jnp.exp(m - m_new)
        pj = jnp.exp(sj - m_new)
        l = alpha * l + jnp.sum(pj, axis=0, keepdims=True)
        acc = alpha * acc + _dot(vt_ref[j], pj.astype(BF16))
        return m_new, l, acc

    _, l, acc = lax.fori_loop(0, qi, body, (m0, l0, acc0))
    o_ref[...] = (acc / l).T.astype(o_ref.dtype)


def _moba(proj, q_gain, k_gain, batch, seq):
    n = batch * seq
    nb = seq // MOBA_BLOCK
    grid = (batch, MOBA_HEADS, nb)
    kcol = MOBA_WIDTH // HEAD_DIM
    return pl.pallas_call(
        _moba_kernel, grid=grid,
        in_specs=[
            pl.BlockSpec((MOBA_BLOCK, HEAD_DIM), lambda b, h, i: (b * nb + i, h)),
            pl.BlockSpec((seq, HEAD_DIM), lambda b, h, i: (b, kcol + h)),
            pl.BlockSpec((seq, HEAD_DIM), lambda b, h, i: (b, 2 * kcol + h)),
            pl.BlockSpec((1, HEAD_DIM), lambda b, h, i: (0, 0)),
            pl.BlockSpec((1, HEAD_DIM), lambda b, h, i: (0, 0)),
        ],
        out_specs=pl.BlockSpec((MOBA_BLOCK, HEAD_DIM), lambda b, h, i: (b * nb + i, h)),
        out_shape=jax.ShapeDtypeStruct((n, MOBA_WIDTH), BF16),
        scratch_shapes=[
            pltpu.VMEM((nb, MOBA_BLOCK, HEAD_DIM), BF16),
            pltpu.VMEM((nb, HEAD_DIM, MOBA_BLOCK), BF16),
            pltpu.VMEM((nb, HEAD_DIM), F32),
            pltpu.VMEM((nb, MOBA_BLOCK), F32),
        ],
        compiler_params=_params("arbitrary", "arbitrary", "arbitrary"), name="moba",
    )(proj, proj, proj, q_gain, k_gain)


def _dot_split(a16, b):
    hi = b.astype(BF16)
    lo = (b - hi.astype(F32)).astype(BF16)
    return _dot(a16, hi) + _dot(a16, lo)


def _dn_kernel(qkv_ref, z_ref, sm_ref, cw_ref, al_ref, dtb_ref, og_ref, o_ref, cs_ref, st_ref):
    c = DN_TILE
    width = qkv_ref.shape[1]

    @pl.when(pl.program_id(1) == 0)
    def _():
        cs_ref[0:8, :] = jnp.zeros((8, width), F32)
        st_ref[...] = jnp.zeros(st_ref.shape, F32)

    cs_ref[8:8 + c, :] = qkv_ref[...].astype(F32)
    y = cw_ref[0:1, :] * cs_ref[5:5 + c, :]
    for i in range(1, DN_CONV):
        y = y + cw_ref[i:i + 1, :] * cs_ref[5 + i:5 + i + c, :]
    cs_ref[0:8, :] = cs_ref[c:c + 8, :]
    y = y * _sigmoid(y)

    sm = sm_ref[...]
    beta_all = _sigmoid(sm)
    xg = sm + dtb_ref[...]
    softplus = jnp.maximum(xg, 0.0) + jnp.log(1.0 + jnp.exp(-jnp.abs(xg)))
    g_all = -jnp.exp(al_ref[...]) * softplus

    ri = lax.broadcasted_iota(jnp.int32, (c, c), 0)
    ci = lax.broadcasted_iota(jnp.int32, (c, c), 1)
    tril = ri >= ci
    strict = ri > ci
    eye = jnp.where(ri == ci, 1.0, 0.0)
    lower_blocks = []
    s = 1
    while s < c:
        same_2s = ((ri ^ ci) & ~(2 * s - 1)) == 0
        lower_blocks.append(jnp.logical_and(same_2s, jnp.logical_and((ri & s) != 0, (ci & s) == 0)))
        s *= 2
    gcum = _dot_split(jnp.where(tril, 1.0, 0.0).astype(BF16), g_all)
    gcum_t = gcum.T

    for h in range(DN_HEADS):
        lo = h * HEAD_DIM
        qh = y[:, lo:lo + HEAD_DIM]
        kh = y[:, DN_WIDTH + lo:DN_WIDTH + lo + HEAD_DIM]
        vh = y[:, 2 * DN_WIDTH + lo:2 * DN_WIDTH + lo + HEAD_DIM]
        qh = qh * lax.rsqrt(jnp.sum(qh * qh, axis=-1, keepdims=True) + NORM_EPS) * (HEAD_DIM ** -0.5)
        kh = kh * lax.rsqrt(jnp.sum(kh * kh, axis=-1, keepdims=True) + NORM_EPS)
        beta = beta_all[:, h:h + 1]
        g_col = gcum[:, DN_HEADS + h:DN_HEADS + h + 1]
        g_row = gcum_t[DN_HEADS + h:DN_HEADS + h + 1, :]
        g_last = g_col[c - 1:c, :]
        decay = jnp.where(tril, jnp.exp(jnp.minimum(g_col - g_row, 0.0)), 0.0)
        e_g = jnp.exp(g_col)

        kb = kh * beta
        k16 = kh.astype(BF16)
        nmat = jnp.where(strict, _dot_nt(kb.astype(BF16), k16) * decay, 0.0)
        attn = _dot_nt(qh.astype(BF16), k16) * decay

        rhs = jnp.concatenate([vh * beta, kb * e_g], axis=1)
        n16 = nmat.astype(BF16)
        tinv = eye - jnp.where(lower_blocks[0], nmat, 0.0)
        for msk in lower_blocks[1:]:
            t16 = tinv.astype(BF16)
            cm = jnp.where(msk, n16, jnp.zeros_like(n16))
            tinv = tinv - _dot(t16, _dot(cm, t16).astype(BF16))
        rhs = _dot(tinv.astype(BF16), rhs.astype(BF16))
        u = rhs[:, :HEAD_DIM]
        w = rhs[:, HEAD_DIM:]

        state = st_ref[h]
        s16 = state.astype(BF16)
        v_new = u - _dot(w.astype(BF16), s16)
        v16 = v_new.astype(BF16)
        o = _dot((qh * e_g).astype(BF16), s16) + _dot(attn.astype(BF16), v16)
        kd = kh * jnp.exp(g_last - g_col)
        st_ref[h] = state * jnp.exp(g_last) + _dot_tn(kd.astype(BF16), v16)

        on = o * lax.rsqrt(jnp.mean(o * o, axis=-1, keepdims=True) + NORM_EPS) * og_ref[...]
        zh = z_ref[:, lo:lo + HEAD_DIM].astype(F32)
        o_ref[:, lo:lo + HEAD_DIM] = (on * (zh * _sigmoid(zh))).astype(o_ref.dtype)


def _deltanet(proj, small, conv_w, a_log_row, dt_bias_row, out_gain, batch, seq):
    n = batch * seq
    nt = seq // DN_TILE
    qkv_w = 3 * DN_WIDTH
    qkv_blk = (3 * MOBA_WIDTH) // qkv_w
    z_blk = (3 * MOBA_WIDTH + qkv_w) // DN_WIDTH
    return pl.pallas_call(
        _dn_kernel, grid=(batch, nt),
        in_specs=[
            pl.BlockSpec((DN_TILE, qkv_w), lambda b, t: (b * nt + t, qkv_blk)),
            pl.BlockSpec((DN_TILE, DN_WIDTH), lambda b, t: (b * nt + t, z_blk)),
            pl.BlockSpec((DN_TILE, SMALL_COLS), lambda b, t: (b * nt + t, 0)),
            pl.BlockSpec((8, qkv_w), lambda b, t: (0, 0)),
            pl.BlockSpec((1, SMALL_COLS), lambda b, t: (0, 0)),
            pl.BlockSpec((1, SMALL_COLS), lambda b, t: (0, 0)),
            pl.BlockSpec((1, HEAD_DIM), lambda b, t: (0, 0)),
        ],
        out_specs=pl.BlockSpec((DN_TILE, DN_WIDTH), lambda b, t: (b * nt + t, 0)),
        out_shape=jax.ShapeDtypeStruct((n, DN_WIDTH), BF16),
        scratch_shapes=[
            pltpu.VMEM((DN_TILE + 8, qkv_w), F32),
            pltpu.VMEM((DN_HEADS, HEAD_DIM, HEAD_DIM), F32),
        ],
        compiler_params=_params("arbitrary", "arbitrary"), name="deltanet",
    )(proj, proj, small, conv_w, a_log_row, dt_bias_row, out_gain)


def _merge_kernel(ya_ref, yb_ref, wa_ref, wb_ref, ga_ref, gb_ref, o_ref):
    a = _dot(ya_ref[...], wa_ref[...])
    b = _dot(yb_ref[...], wb_ref[...])
    ga = _sigmoid(ga_ref[...].astype(F32))
    gb = _sigmoid(gb_ref[...].astype(F32))
    o_ref[...] = (ga * a + gb * b).astype(o_ref.dtype)


def _merge(y_a, y_b, w_a, w_b, proj):
    n = y_a.shape[0]
    tm = min(MM_TM, n)
    ga_blk = (3 * MOBA_WIDTH + 4 * DN_WIDTH) // MM_TN
    gb_blk = ga_blk + D_MODEL // MM_TN
    return pl.pallas_call(
        _merge_kernel, grid=(n // tm, D_MODEL // MM_TN),
        in_specs=[
            pl.BlockSpec((tm, MOBA_WIDTH), lambda i, j: (i, 0)),
            pl.BlockSpec((tm, DN_WIDTH), lambda i, j: (i, 0)),
            pl.BlockSpec((MOBA_WIDTH, MM_TN), lambda i, j: (0, j)),
            pl.BlockSpec((DN_WIDTH, MM_TN), lambda i, j: (0, j)),
            pl.BlockSpec((tm, MM_TN), lambda i, j: (i, ga_blk + j)),
            pl.BlockSpec((tm, MM_TN), lambda i, j: (i, gb_blk + j)),
        ],
        out_specs=pl.BlockSpec((tm, MM_TN), lambda i, j: (i, j)),
        out_shape=jax.ShapeDtypeStruct((n, D_MODEL), BF16),
        compiler_params=_params("arbitrary", "arbitrary"), name="merge",
    )(y_a, y_b, w_a, w_b, proj, proj)


def _res_mm_kernel(a_ref, w_ref, x_ref, o_ref):
    o_ref[...] = x_ref[...] + _dot(a_ref[...], w_ref[...])


def _res_mm(a, w, x2, tn):
    n, k = a.shape
    cols = w.shape[1]
    tm = min(MM_TM, n)
    return pl.pallas_call(
        _res_mm_kernel, grid=(n // tm, cols // tn),
        in_specs=[
            pl.BlockSpec((tm, k), lambda i, j: (i, 0)),
            pl.BlockSpec((k, tn), lambda i, j: (0, j)),
            pl.BlockSpec((tm, tn), lambda i, j: (i, j)),
        ],
        out_specs=pl.BlockSpec((tm, tn), lambda i, j: (i, j)),
        out_shape=jax.ShapeDtypeStruct((n, cols), F32),
        compiler_params=_params("arbitrary", "arbitrary"), name="res_mm",
    )(a, w, x2)


def _ffn_in_kernel(tiles_per_seq, x_ref, g_ref, wg_ref, wu_ref, cw_ref, cb_ref, o_ref,
                   h_ref, cs_ref, tail_ref):
    i = pl.program_id(0)
    j = pl.program_id(1)
    tm = x_ref.shape[0]

    @pl.when(j == 0)
    def _():
        _rms_rows(x_ref, g_ref, h_ref)

    h = h_ref[...]
    gate = _dot(h, wg_ref[...])
    up = _dot(h, wu_ref[...])

    seq_start = (i % tiles_per_seq) == 0

    @pl.when(seq_start)
    def _():
        cs_ref[0:8, :] = jnp.zeros((8, cs_ref.shape[1]), F32)

    @pl.when(jnp.logical_not(seq_start))
    def _():
        cs_ref[0:8, :] = tail_ref[j]

    cs_ref[8:8 + tm, :] = gate
    tail_ref[j] = gate[tm - 8:, :]
    y = cb_ref[...] + cw_ref[0:1, :] * cs_ref[6:6 + tm, :]
    for t in range(1, FFN_CONV):
        y = y + cw_ref[t:t + 1, :] * cs_ref[6 + t:6 + t + tm, :]
    o_ref[...] = (y * _sigmoid(y) * up).astype(o_ref.dtype)


def _ffn_in(x2, gain, w_in16, conv_w, conv_b, seq):
    n, d = x2.shape
    tm = min(MM_TM, seq)
    nj = D_FF // FFN_TN
    return pl.pallas_call(
        functools.partial(_ffn_in_kernel, seq // tm),
        grid=(n // tm, nj),
        in_specs=[
            pl.BlockSpec((tm, d), lambda i, j: (i, 0)),
            pl.BlockSpec((1, d), lambda i, j: (0, 0)),
            pl.BlockSpec((d, FFN_TN), lambda i, j: (0, j)),
            pl.BlockSpec((d, FFN_TN), lambda i, j: (0, nj + j)),
            pl.BlockSpec((8, FFN_TN), lambda i, j: (0, j)),
            pl.BlockSpec((1, FFN_TN), lambda i, j: (0, j)),
        ],
        out_specs=pl.BlockSpec((tm, FFN_TN), lambda i, j: (i, j)),
        out_shape=jax.ShapeDtypeStruct((n, D_FF), BF16),
        scratch_shapes=[
            pltpu.VMEM((tm, d), BF16),
            pltpu.VMEM((tm + 8, FFN_TN), F32),
            pltpu.VMEM((nj, 8, FFN_TN), F32),
        ],
        compiler_params=_params("arbitrary", "arbitrary"), name="ffn_in",
    )(x2, gain, w_in16, w_in16, conv_w, conv_b)


def _ffn(x2, gain, w_ffn_in, conv_w, conv_b, w_down, seq):
    act = _ffn_in(x2, gain[None, :], w_ffn_in.astype(BF16), _pad_rows(conv_w, 8), conv_b[None, :], seq)
    return _res_mm(act, w_down.astype(BF16), x2, FFN_TN)


def _pad_rows(w, rows):
    return jnp.pad(w, ((0, rows - w.shape[0]), (0, 0)))


def _lane_row(v, offset):
    return jnp.zeros((1, SMALL_COLS), F32).at[0, offset:offset + v.shape[0]].set(v.astype(F32))


def kernel(x, attn_norm, w_in, moba_q_norm, moba_k_norm, dn_conv, dn_a_log, dn_dt_bias, dn_out_norm,
           w_branch_a, w_branch_b, w_out, ffn_norm, w_ffn_in, ffn_conv, ffn_conv_bias, w_ffn_down):
    batch, seq, d = x.shape
    depth = w_in.shape[0]
    n = batch * seq
    assert d == D_MODEL and seq % MOBA_BLOCK == 0 and seq % DN_TILE == 0 and n % min(MM_TM, n) == 0
    main_w = 3 * MOBA_WIDTH + 4 * DN_WIDTH
    small_w = 2 * DN_HEADS
    x2 = x.reshape(n, d)
    for l in range(depth):
        w_main = jnp.concatenate([w_in[l, :, :main_w], w_in[l, :, main_w + small_w:]], axis=1).astype(BF16)
        w_small = jnp.pad(w_in[l, :, main_w:main_w + small_w],
                          ((0, 0), (0, SMALL_COLS - small_w))).astype(BF16)
        proj, small = _norm_proj(x2, attn_norm[l][None, :], w_main, w_small)
        y_a = _moba(proj, moba_q_norm[l][None, :], moba_k_norm[l][None, :], batch, seq)
        y_b = _deltanet(proj, small, _pad_rows(dn_conv[l], 8), _lane_row(dn_a_log[l], DN_HEADS),
                        _lane_row(dn_dt_bias[l], DN_HEADS), dn_out_norm[l][None, :], batch, seq)
        merged = _merge(y_a, y_b, w_branch_a[l].astype(BF16), w_branch_b[l].astype(BF16), proj)
        x2 = _res_mm(merged, w_out[l].astype(BF16), x2, MM_TN)
        x2 = _ffn(x2, ffn_norm[l], w_ffn_in[l], ffn_conv[l], ffn_conv_bias[l], w_ffn_down[l], seq)
    return x2.reshape(batch, seq, d)
```

```python
import functools

import jax
import jax.numpy as jnp
from jax import lax
from jax.experimental import pallas as pl
from jax.experimental.pallas import tpu as pltpu

F32 = jnp.float32
BF16 = jnp.bfloat16

D_MODEL = 2048
HEAD_DIM = 128
MOBA_HEADS = 8
MOBA_WIDTH = MOBA_HEADS * HEAD_DIM
MOBA_BLOCK = 256
MOBA_TOPK = 3
DN_HEADS = 8
DN_WIDTH = DN_HEADS * HEAD_DIM
DN_CONV = 4
D_FF = 5632
FFN_CONV = 3
NORM_EPS = 1e-6

PROJ_COLS = 3 * MOBA_WIDTH + 3 * DN_WIDTH + DN_WIDTH + 2 * D_MODEL
SMALL_COLS = 128

LANES = 128
VMEM_LIMIT_BYTES = 56 * 1024 * 1024

MM_TM = 1024
MM_TN = 1024
NORM_ROWS = 128
MOBA_GROUP = 4
DN_TILE = 128
DN_HIST = 16
FFN_TN = 512
NEG = -1e30


def _sigmoid(x):
    return 1.0 / (1.0 + jnp.exp(-x))


def _dot(a, b):
    return jnp.dot(a, b, preferred_element_type=F32)


def _dot_nt(a, b, precision=None):
    return lax.dot_general(a, b, (((1,), (1,)), ((), ())), precision=precision,
                           preferred_element_type=F32)


def _dot_tn(a, b):
    return lax.dot_general(a, b, (((0,), (0,)), ((), ())), preferred_element_type=F32)


def _params(*sem):
    return pltpu.CompilerParams(dimension_semantics=sem, vmem_limit_bytes=VMEM_LIMIT_BYTES)


def _rms_rows(x_ref, g_ref, h_ref):
    rows = x_ref.shape[0]

    def body(c, carry):
        r = pl.multiple_of(c * NORM_ROWS, NORM_ROWS)
        x = x_ref[pl.ds(r, NORM_ROWS), :]
        ms = jnp.mean(x * x, axis=-1, keepdims=True)
        h_ref[pl.ds(r, NORM_ROWS), :] = (x * lax.rsqrt(ms + NORM_EPS) * g_ref[...]).astype(h_ref.dtype)
        return carry

    lax.fori_loop(0, rows // NORM_ROWS, body, 0)


def _norm_proj_small_kernel(x_ref, g_ref, w_ref, ws_ref, o_ref, s_ref, h_ref):
    @pl.when(pl.program_id(1) == 0)
    def _():
        _rms_rows(x_ref, g_ref, h_ref)
        s_ref[...] = _dot(h_ref[...], ws_ref[...])

    o_ref[...] = _dot(h_ref[...], w_ref[...]).astype(o_ref.dtype)


def _norm_proj_kernel(x_ref, g_ref, w_ref, o_ref, h_ref):
    @pl.when(pl.program_id(1) == 0)
    def _():
        _rms_rows(x_ref, g_ref, h_ref)

    o_ref[...] = _dot(h_ref[...], w_ref[...]).astype(o_ref.dtype)


def _norm_proj(x2, gain, w, w_small=None):
    n, d = x2.shape
    cols = w.shape[1]
    tm = min(MM_TM, n)
    grid = (n // tm, cols // MM_TN)
    in_specs = [
        pl.BlockSpec((tm, d), lambda i, j: (i, 0)),
        pl.BlockSpec((1, d), lambda i, j: (0, 0)),
        pl.BlockSpec((d, MM_TN), lambda i, j: (0, j)),
    ]
    out_main = jax.ShapeDtypeStruct((n, cols), BF16)
    spec_main = pl.BlockSpec((tm, MM_TN), lambda i, j: (i, j))
    scratch = [pltpu.VMEM((tm, d), BF16)]
    if w_small is None:
        return pl.pallas_call(
            _norm_proj_kernel, grid=grid, in_specs=in_specs, out_specs=spec_main,
            out_shape=out_main, scratch_shapes=scratch,
            compiler_params=_params("arbitrary", "arbitrary"), name="norm_proj",
        )(x2, gain, w)
    in_specs.append(pl.BlockSpec((d, SMALL_COLS), lambda i, j: (0, 0)))
    return pl.pallas_call(
        _norm_proj_small_kernel, grid=grid, in_specs=in_specs,
        out_specs=[spec_main, pl.BlockSpec((tm, SMALL_COLS), lambda i, j: (i, 0))],
        out_shape=[out_main, jax.ShapeDtypeStruct((n, SMALL_COLS), F32)],
        scratch_shapes=scratch,
        compiler_params=_params("arbitrary", "arbitrary"), name="norm_proj_small",
    )(x2, gain, w, w_small)


def _moba_kernel(q_ref, k_ref, v_ref, qg_ref, kg_ref, o_ref, kn_ref, vt_ref, km_ref, sel_ref):
    qi = pl.program_id(2)
    hp, nb = kn_ref.shape[0], kn_ref.shape[1]
    blk = MOBA_BLOCK
    heads = range(hp)

    def cols(h):
        return slice(h * HEAD_DIM, (h + 1) * HEAD_DIM)

    @pl.when(qi == 0)
    def _():
        def prep(n, carry):
            r = pl.multiple_of(n * blk, blk)
            for h in heads:
                kb = k_ref[pl.ds(r, blk), cols(h)].astype(F32)
                ms = jnp.mean(kb * kb, axis=-1, keepdims=True)
                kn = kb * lax.rsqrt(ms + NORM_EPS) * kg_ref[...]
                kn_ref[h, n] = kn.astype(BF16)
                km_ref[h, pl.ds(n, 1), :] = jnp.mean(kn, axis=0, keepdims=True)
                vt_ref[h, n] = v_ref[pl.ds(r, blk), cols(h)].astype(F32).T.astype(BF16)
            return carry

        lax.fori_loop(0, nb, prep, 0)

    q = [q_ref[:, cols(h)].astype(F32) for h in heads]
    qn = [t * lax.rsqrt(jnp.mean(t * t, axis=-1, keepdims=True) + NORM_EPS) * qg_ref[...] for t in q]

    gate = [_dot_nt(km_ref[h], qn[h], precision=lax.Precision.HIGHEST) for h in heads]
    rows = lax.broadcasted_iota(jnp.int32, (nb, blk), 0).astype(F32)
    past = rows < qi.astype(F32)
    gate = [jnp.where(past, g, NEG) for g in gate]
    sel = [jnp.zeros((nb, blk), F32) for _ in heads]
    for _ in range(MOBA_TOPK):
        mx = [jnp.max(g, axis=0, keepdims=True) for g in gate]
        idx = [jnp.min(jnp.where(gate[h] == mx[h], rows, float(nb)), axis=0, keepdims=True) for h in heads]
        pick = [jnp.logical_and(rows == idx[h], mx[h] > 0.5 * NEG) for h in heads]
        sel = [jnp.where(pick[h], 1.0, sel[h]) for h in heads]
        gate = [jnp.where(pick[h], NEG, gate[h]) for h in heads]
    for h in heads:
        sel_ref[h] = sel[h]

    qb = [(t * (HEAD_DIM ** -0.5)).astype(BF16) for t in qn]

    s = [_dot_nt(kn_ref[h, qi], qb[h]) for h in heads]
    kidx = lax.broadcasted_iota(jnp.int32, (blk, blk), 0)
    qidx = lax.broadcasted_iota(jnp.int32, (blk, blk), 1)
    causal = kidx <= qidx
    s = [jnp.where(causal, t, NEG) for t in s]
    m0 = [jnp.max(t, axis=0, keepdims=True) for t in s]
    p = [jnp.exp(s[h] - m0[h]) for h in heads]
    l0 = [jnp.sum(t, axis=0, keepdims=True) for t in p]
    acc0 = [_dot(vt_ref[h, qi], p[h].astype(BF16)) for h in heads]

    def body(j, carry):
        m, l, acc = carry
        sj = [_dot_nt(kn_ref[h, j], qb[h]) for h in heads]
        sj = [jnp.where(sel_ref[h, pl.ds(j, 1), :] > 0.0, sj[h], NEG) for h in heads]
        m_new = [jnp.maximum(m[h], jnp.max(sj[h], axis=0, keepdims=True)) for h in heads]
        alpha = [jnp.exp(m[h] - m_new[h]) for h in heads]
        pj = [jnp.exp(sj[h] - m_new[h]) for h in heads]
        l = [alpha[h] * l[h] + jnp.sum(pj[h], axis=0, keepdims=True) for h in heads]
        acc = [alpha[h] * acc[h] + _dot(vt_ref[h, j], pj[h].astype(BF16)) for h in heads]
        return m_new, l, acc

    _, l, acc = lax.fori_loop(0, qi, body, (m0, l0, acc0))
    for h in heads:
        o_ref[:, cols(h)] = (acc[h] / l[h]).T.astype(o_ref.dtype)


def _moba(proj, q_gain, k_gain, batch, seq):
    n = batch * seq
    nb = seq // MOBA_BLOCK
    hp = MOBA_GROUP
    gw = hp * HEAD_DIM
    groups = MOBA_HEADS // hp
    return pl.pallas_call(
        _moba_kernel, grid=(batch, groups, nb),
        in_specs=[
            pl.BlockSpec((MOBA_BLOCK, gw), lambda b, g, i: (b * nb + i, g)),
            pl.BlockSpec((seq, gw), lambda b, g, i: (b, groups + g)),
            pl.BlockSpec((seq, gw), lambda b, g, i: (b, 2 * groups + g)),
            pl.BlockSpec((1, HEAD_DIM), lambda b, g, i: (0, 0)),
            pl.BlockSpec((1, HEAD_DIM), lambda b, g, i: (0, 0)),
        ],
        out_specs=pl.BlockSpec((MOBA_BLOCK, gw), lambda b, g, i: (b * nb + i, g)),
        out_shape=jax.ShapeDtypeStruct((n, MOBA_WIDTH), BF16),
        scratch_shapes=[
            pltpu.VMEM((hp, nb, MOBA_BLOCK, HEAD_DIM), BF16),
            pltpu.VMEM((hp, nb, HEAD_DIM, MOBA_BLOCK), BF16),
            pltpu.VMEM((hp, nb, HEAD_DIM), F32),
            pltpu.VMEM((hp, nb, MOBA_BLOCK), F32),
        ],
        compiler_params=_params("arbitrary", "arbitrary", "arbitrary"), name="moba",
    )(proj, proj, proj, q_gain, k_gain)


def _dot_split(a16, b):
    hi = b.astype(BF16)
    lo = (b - hi.astype(F32)).astype(BF16)
    return _dot(a16, hi) + _dot(a16, lo)


def _dn_kernel(qkv_ref, z_ref, sm_ref, cw_ref, al_ref, dtb_ref, og_ref, o_ref, xe_ref, st_ref):
    c = DN_TILE
    width = qkv_ref.shape[1]

    @pl.when(pl.program_id(1) == 0)
    def _():
        xe_ref[0:DN_HIST, :] = jnp.zeros((DN_HIST, width), BF16)
        st_ref[...] = jnp.zeros(st_ref.shape, F32)

    xe_ref[DN_HIST:DN_HIST + c, :] = qkv_ref[...]
    xe = xe_ref[...]
    sr = lax.broadcasted_iota(jnp.int32, (c, DN_HIST + c), 0)
    sc = lax.broadcasted_iota(jnp.int32, (c, DN_HIST + c), 1)
    y = cw_ref[DN_CONV - 1:DN_CONV, :] * qkv_ref[...].astype(F32)
    for d in range(1, DN_CONV):
        shift = jnp.where(sc == sr + (DN_HIST - d), 1.0, 0.0).astype(BF16)
        y = y + cw_ref[DN_CONV - 1 - d:DN_CONV - d, :] * _dot(shift, xe)
    xe_ref[0:DN_HIST, :] = xe_ref[c:c + DN_HIST, :]
    y = y * _sigmoid(y)

    sm = sm_ref[...]
    beta_all = _sigmoid(sm)
    xg = sm + dtb_ref[...]
    softplus = jnp.maximum(xg, 0.0) + jnp.log(1.0 + jnp.exp(-jnp.abs(xg)))
    g_all = -jnp.exp(al_ref[...]) * softplus

    ri = lax.broadcasted_iota(jnp.int32, (c, c), 0)
    ci = lax.broadcasted_iota(jnp.int32, (c, c), 1)
    tril = ri >= ci
    strict = ri > ci
    eye = jnp.where(ri == ci, 1.0, 0.0)
    lower_blocks = []
    s = 1
    while s < c:
        same_2s = ((ri ^ ci) & ~(2 * s - 1)) == 0
        lower_blocks.append(jnp.logical_and(same_2s, jnp.logical_and((ri & s) != 0, (ci & s) == 0)))
        s *= 2
    gcum = _dot_split(jnp.where(tril, 1.0, 0.0).astype(BF16), g_all)
    gcum_t = gcum.T

    heads = range(DN_HEADS)

    def head_cols(base, h):
        return y[:, base + h * HEAD_DIM:base + (h + 1) * HEAD_DIM]

    q = [head_cols(0, h) for h in heads]
    k = [head_cols(DN_WIDTH, h) for h in heads]
    v = [head_cols(2 * DN_WIDTH, h) for h in heads]
    q = [t * lax.rsqrt(jnp.sum(t * t, axis=-1, keepdims=True) + NORM_EPS) * (HEAD_DIM ** -0.5) for t in q]
    k = [t * lax.rsqrt(jnp.sum(t * t, axis=-1, keepdims=True) + NORM_EPS) for t in k]
    beta = [beta_all[:, h:h + 1] for h in heads]
    g_col = [gcum[:, DN_HEADS + h:DN_HEADS + h + 1] for h in heads]
    g_row = [gcum_t[DN_HEADS + h:DN_HEADS + h + 1, :] for h in heads]
    g_last = [t[c - 1:c, :] for t in g_col]
    decay = [jnp.where(tril, jnp.exp(jnp.minimum(g_col[h] - g_row[h], 0.0)), 0.0) for h in heads]
    e_g = [jnp.exp(t) for t in g_col]
    kb = [k[h] * beta[h] for h in heads]
    k16 = [t.astype(BF16) for t in k]
    nmat = [jnp.where(strict, _dot_nt(kb[h].astype(BF16), k16[h]) * decay[h], 0.0) for h in heads]
    attn = [(_dot_nt(q[h].astype(BF16), k16[h]) * decay[h]).astype(BF16) for h in heads]

    rhs = [jnp.concatenate([v[h] * beta[h], kb[h] * e_g[h]], axis=1).astype(BF16) for h in heads]
    n16 = [t.astype(BF16) for t in nmat]
    tinv = [eye - jnp.where(lower_blocks[0], t, 0.0) for t in nmat]
    for msk in lower_blocks[1:]:
        t16 = [t.astype(BF16) for t in tinv]
        cm = [jnp.where(msk, t, jnp.zeros_like(t)) for t in n16]
        p = [_dot(cm[h], t16[h]).astype(BF16) for h in heads]
        tinv = [tinv[h] - _dot(t16[h], p[h]) for h in heads]
    sol = [_dot(tinv[h].astype(BF16), rhs[h]) for h in heads]

    state = [st_ref[h] for h in heads]
    s16 = [t.astype(BF16) for t in state]
    v16 = [(sol[h][:, :HEAD_DIM] - _dot(sol[h][:, HEAD_DIM:].astype(BF16), s16[h])).astype(BF16)
           for h in heads]
    o = [_dot((q[h] * e_g[h]).astype(BF16), s16[h]) + _dot(attn[h], v16[h]) for h in heads]
    kd = [(k[h] * jnp.exp(g_last[h] - g_col[h])).astype(BF16) for h in heads]
    for h in heads:
        st_ref[h] = state[h] * jnp.exp(g_last[h]) + _dot_tn(kd[h], v16[h])
    for h in heads:
        on = o[h] * lax.rsqrt(jnp.mean(o[h] * o[h], axis=-1, keepdims=True) + NORM_EPS) * og_ref[...]
        zh = z_ref[:, h * HEAD_DIM:(h + 1) * HEAD_DIM].astype(F32)
        o_ref[:, h * HEAD_DIM:(h + 1) * HEAD_DIM] = (on * (zh * _sigmoid(zh))).astype(o_ref.dtype)


def _deltanet(proj, small, conv_w, a_log_row, dt_bias_row, out_gain, batch, seq):
    n = batch * seq
    nt = seq // DN_TILE
    qkv_w = 3 * DN_WIDTH
    qkv_blk = (3 * MOBA_WIDTH) // qkv_w
    z_blk = (3 * MOBA_WIDTH + qkv_w) // DN_WIDTH
    return pl.pallas_call(
        _dn_kernel, grid=(batch, nt),
        in_specs=[
            pl.BlockSpec((DN_TILE, qkv_w), lambda b, t: (b * nt + t, qkv_blk)),
            pl.BlockSpec((DN_TILE, DN_WIDTH), lambda b, t: (b * nt + t, z_blk)),
            pl.BlockSpec((DN_TILE, SMALL_COLS), lambda b, t: (b * nt + t, 0)),
            pl.BlockSpec((8, qkv_w), lambda b, t: (0, 0)),
            pl.BlockSpec((1, SMALL_COLS), lambda b, t: (0, 0)),
            pl.BlockSpec((1, SMALL_COLS), lambda b, t: (0, 0)),
            pl.BlockSpec((1, HEAD_DIM), lambda b, t: (0, 0)),
        ],
        out_specs=pl.BlockSpec((DN_TILE, DN_WIDTH), lambda b, t: (b * nt + t, 0)),
        out_shape=jax.ShapeDtypeStruct((n, DN_WIDTH), BF16),
        scratch_shapes=[
            pltpu.VMEM((DN_HIST + DN_TILE, qkv_w), BF16),
            pltpu.VMEM((DN_HEADS, HEAD_DIM, HEAD_DIM), F32),
        ],
        compiler_params=_params("arbitrary", "arbitrary"), name="deltanet",
    )(proj, proj, small, conv_w, a_log_row, dt_bias_row, out_gain)


def _merge_kernel(ya_ref, yb_ref, wa_ref, wb_ref, ga_ref, gb_ref, o_ref):
    a = _dot(ya_ref[...], wa_ref[...])
    b = _dot(yb_ref[...], wb_ref[...])
    ga = _sigmoid(ga_ref[...].astype(F32))
    gb = _sigmoid(gb_ref[...].astype(F32))
    o_ref[...] = (ga * a + gb * b).astype(o_ref.dtype)


def _merge(y_a, y_b, w_a, w_b, proj):
    n = y_a.shape[0]
    tm = min(MM_TM, n)
    ga_blk = (3 * MOBA_WIDTH + 4 * DN_WIDTH) // MM_TN
    gb_blk = ga_blk + D_MODEL // MM_TN
    return pl.pallas_call(
        _merge_kernel, grid=(n // tm, D_MODEL // MM_TN),
        in_specs=[
            pl.BlockSpec((tm, MOBA_WIDTH), lambda i, j: (i, 0)),
            pl.BlockSpec((tm, DN_WIDTH), lambda i, j: (i, 0)),
            pl.BlockSpec((MOBA_WIDTH, MM_TN), lambda i, j: (0, j)),
            pl.BlockSpec((DN_WIDTH, MM_TN), lambda i, j: (0, j)),
            pl.BlockSpec((tm, MM_TN), lambda i, j: (i, ga_blk + j)),
            pl.BlockSpec((tm, MM_TN), lambda i, j: (i, gb_blk + j)),
        ],
        out_specs=pl.BlockSpec((tm, MM_TN), lambda i, j: (i, j)),
        out_shape=jax.ShapeDtypeStruct((n, D_MODEL), BF16),
        compiler_params=_params("arbitrary", "arbitrary"), name="merge",
    )(y_a, y_b, w_a, w_b, proj, proj)


def _res_mm_kernel(a_ref, w_ref, x_ref, o_ref):
    o_ref[...] = x_ref[...] + _dot(a_ref[...], w_ref[...])


def _res_mm(a, w, x2, tn):
    n, k = a.shape
    cols = w.shape[1]
    tm = min(MM_TM, n)
    return pl.pallas_call(
        _res_mm_kernel, grid=(n // tm, cols // tn),
        in_specs=[
            pl.BlockSpec((tm, k), lambda i, j: (i, 0)),
            pl.BlockSpec((k, tn), lambda i, j: (0, j)),
            pl.BlockSpec((tm, tn), lambda i, j: (i, j)),
        ],
        out_specs=pl.BlockSpec((tm, tn), lambda i, j: (i, j)),
        out_shape=jax.ShapeDtypeStruct((n, cols), F32),
        compiler_params=_params("arbitrary", "arbitrary"), name="res_mm",
    )(a, w, x2)


def _ffn_in_kernel(tiles_per_seq, x_ref, g_ref, wg_ref, wu_ref, cw_ref, cb_ref, o_ref,
                   h_ref, cs_ref, tail_ref):
    i = pl.program_id(0)
    j = pl.program_id(1)
    tm = x_ref.shape[0]

    @pl.when(j == 0)
    def _():
        _rms_rows(x_ref, g_ref, h_ref)

    h = h_ref[...]
    gate = _dot(h, wg_ref[...])
    up = _dot(h, wu_ref[...])

    seq_start = (i % tiles_per_seq) == 0

    @pl.when(seq_start)
    def _():
        cs_ref[0:8, :] = jnp.zeros((8, cs_ref.shape[1]), F32)

    @pl.when(jnp.logical_not(seq_start))
    def _():
        cs_ref[0:8, :] = tail_ref[j]

    cs_ref[8:8 + tm, :] = gate
    tail_ref[j] = gate[tm - 8:, :]
    y = cb_ref[...] + cw_ref[0:1, :] * cs_ref[6:6 + tm, :]
    for t in range(1, FFN_CONV):
        y = y + cw_ref[t:t + 1, :] * cs_ref[6 + t:6 + t + tm, :]
    o_ref[...] = (y * _sigmoid(y) * up).astype(o_ref.dtype)


def _ffn_in(x2, gain, w_in16, conv_w, conv_b, seq):
    n, d = x2.shape
    tm = min(MM_TM, seq)
    nj = D_FF // FFN_TN
    return pl.pallas_call(
        functools.partial(_ffn_in_kernel, seq // tm),
        grid=(n // tm, nj),
        in_specs=[
            pl.BlockSpec((tm, d), lambda i, j: (i, 0)),
            pl.BlockSpec((1, d), lambda i, j: (0, 0)),
            pl.BlockSpec((d, FFN_TN), lambda i, j: (0, j)),
            pl.BlockSpec((d, FFN_TN), lambda i, j: (0, nj + j)),
            pl.BlockSpec((8, FFN_TN), lambda i, j: (0, j)),
            pl.BlockSpec((1, FFN_TN), lambda i, j: (0, j)),
        ],
        out_specs=pl.BlockSpec((tm, FFN_TN), lambda i, j: (i, j)),
        out_shape=jax.ShapeDtypeStruct((n, D_FF), BF16),
        scratch_shapes=[
            pltpu.VMEM((tm, d), BF16),
            pltpu.VMEM((tm + 8, FFN_TN), F32),
            pltpu.VMEM((nj, 8, FFN_TN), F32),
        ],
        compiler_params=_params("arbitrary", "arbitrary"), name="ffn_in",
    )(x2, gain, w_in16, w_in16, conv_w, conv_b)


def _ffn(x2, gain, w_ffn_in, conv_w, conv_b, w_down, seq):
    act = _ffn_in(x2, gain[None, :], w_ffn_in.astype(BF16), _pad_rows(conv_w, 8), conv_b[None, :], seq)
    return _res_mm(act, w_down.astype(BF16), x2, FFN_TN)


def _pad_rows(w, rows):
    return jnp.pad(w, ((0, rows - w.shape[0]), (0, 0)))


def _lane_row(v, offset):
    return jnp.zeros((1, SMALL_COLS), F32).at[0, offset:offset + v.shape[0]].set(v.astype(F32))


def kernel(x, attn_norm, w_in, moba_q_norm, moba_k_norm, dn_conv, dn_a_log, dn_dt_bias, dn_out_norm,
           w_branch_a, w_branch_b, w_out, ffn_norm, w_ffn_in, ffn_conv, ffn_conv_bias, w_ffn_down):
    batch, seq, d = x.shape
    depth = w_in.shape[0]
    n = batch * seq
    assert d == D_MODEL and seq % MOBA_BLOCK == 0 and seq % DN_TILE == 0 and n % min(MM_TM, n) == 0
    main_w = 3 * MOBA_WIDTH + 4 * DN_WIDTH
    small_w = 2 * DN_HEADS
    x2 = x.reshape(n, d)
    for l in range(depth):
        w_main = jnp.concatenate([w_in[l, :, :main_w], w_in[l, :, main_w + small_w:]], axis=1).astype(BF16)
        w_small = jnp.pad(w_in[l, :, main_w:main_w + small_w],
                          ((0, 0), (0, SMALL_COLS - small_w))).astype(BF16)
        proj, small = _norm_proj(x2, attn_norm[l][None, :], w_main, w_small)
        y_a = _moba(proj, moba_q_norm[l][None, :], moba_k_norm[l][None, :], batch, seq)
        y_b = _deltanet(proj, small, _pad_rows(dn_conv[l], 8), _lane_row(dn_a_log[l], DN_HEADS),
                        _lane_row(dn_dt_bias[l], DN_HEADS), dn_out_norm[l][None, :], batch, seq)
        merged = _merge(y_a, y_b, w_branch_a[l].astype(BF16), w_branch_b[l].astype(BF16), proj)
        x2 = _res_mm(merged, w_out[l].astype(BF16), x2, MM_TN)
        x2 = _ffn(x2, ffn_norm[l], w_ffn_in[l], ffn_conv[l], ffn_conv_bias[l], w_ffn_down[l], seq)
    return x2.reshape(batch, seq, d)
```

```python
import functools

import jax
import jax.numpy as jnp
from jax import lax
from jax.experimental import pallas as pl
from jax.experimental.pallas import tpu as pltpu

F32 = jnp.float32
BF16 = jnp.bfloat16

D_MODEL = 2048
HEAD_DIM = 128
MOBA_HEADS = 8
MOBA_WIDTH = MOBA_HEADS * HEAD_DIM
MOBA_BLOCK = 256
MOBA_TOPK = 3
DN_HEADS = 8
DN_WIDTH = DN_HEADS * HEAD_DIM
DN_CONV = 4
D_FF = 5632
FFN_CONV = 3
NORM_EPS = 1e-6

PROJ_COLS = 3 * MOBA_WIDTH + 3 * DN_WIDTH + DN_WIDTH + 2 * D_MODEL
SMALL_COLS = 128

LANES = 128
VMEM_LIMIT_BYTES = 56 * 1024 * 1024

MM_TM = 1024
MM_TN = 1024
NORM_ROWS = 128
MOBA_GROUP = 4
MOBA_KV = 2
DN_TILE = 128
DN_HIST = 16
FFN_TN = 512
FFN_ROWS = 256
NEG = -1e30


def _sigmoid(x):
    return 1.0 / (1.0 + jnp.exp(-x))


def _dot(a, b):
    return jnp.dot(a, b, preferred_element_type=F32)


def _dot_nt(a, b, precision=None):
    return lax.dot_general(a, b, (((1,), (1,)), ((), ())), precision=precision,
                           preferred_element_type=F32)


def _dot_tn(a, b):
    return lax.dot_general(a, b, (((0,), (0,)), ((), ())), preferred_element_type=F32)


def _params(*sem):
    return pltpu.CompilerParams(dimension_semantics=sem, vmem_limit_bytes=VMEM_LIMIT_BYTES)


def _rms_rows(x_ref, g_ref, h_ref):
    rows = x_ref.shape[0]

    def body(c, carry):
        r = pl.multiple_of(c * NORM_ROWS, NORM_ROWS)
        x = x_ref[pl.ds(r, NORM_ROWS), :]
        ms = jnp.mean(x * x, axis=-1, keepdims=True)
        h_ref[pl.ds(r, NORM_ROWS), :] = (x * lax.rsqrt(ms + NORM_EPS) * g_ref[...]).astype(h_ref.dtype)
        return carry

    lax.fori_loop(0, rows // NORM_ROWS, body, 0)


def _norm_proj_small_kernel(x_ref, g_ref, w_ref, ws_ref, o_ref, s_ref, h_ref):
    @pl.when(pl.program_id(1) == 0)
    def _():
        _rms_rows(x_ref, g_ref, h_ref)
        s_ref[...] = _dot(h_ref[...], ws_ref[...])

    o_ref[...] = _dot(h_ref[...], w_ref[...]).astype(o_ref.dtype)


def _norm_proj_kernel(x_ref, g_ref, w_ref, o_ref, h_ref):
    @pl.when(pl.program_id(1) == 0)
    def _():
        _rms_rows(x_ref, g_ref, h_ref)

    o_ref[...] = _dot(h_ref[...], w_ref[...]).astype(o_ref.dtype)


def _norm_proj(x2, gain, w, w_small=None):
    n, d = x2.shape
    cols = w.shape[1]
    tm = min(MM_TM, n)
    grid = (n // tm, cols // MM_TN)
    in_specs = [
        pl.BlockSpec((tm, d), lambda i, j: (i, 0)),
        pl.BlockSpec((1, d), lambda i, j: (0, 0)),
        pl.BlockSpec((d, MM_TN), lambda i, j: (0, j)),
    ]
    out_main = jax.ShapeDtypeStruct((n, cols), BF16)
    spec_main = pl.BlockSpec((tm, MM_TN), lambda i, j: (i, j))
    scratch = [pltpu.VMEM((tm, d), BF16)]
    if w_small is None:
        return pl.pallas_call(
            _norm_proj_kernel, grid=grid, in_specs=in_specs, out_specs=spec_main,
            out_shape=out_main, scratch_shapes=scratch,
            compiler_params=_params("arbitrary", "arbitrary"), name="norm_proj",
        )(x2, gain, w)
    in_specs.append(pl.BlockSpec((d, SMALL_COLS), lambda i, j: (0, 0)))
    return pl.pallas_call(
        _norm_proj_small_kernel, grid=grid, in_specs=in_specs,
        out_specs=[spec_main, pl.BlockSpec((tm, SMALL_COLS), lambda i, j: (i, 0))],
        out_shape=[out_main, jax.ShapeDtypeStruct((n, SMALL_COLS), F32)],
        scratch_shapes=scratch,
        compiler_params=_params("arbitrary", "arbitrary"), name="norm_proj_small",
    )(x2, gain, w, w_small)


def _moba_kernel(q_ref, k_ref, v_ref, qg_ref, kg_ref, o_ref, kn_ref, vt_ref, km_ref, sel_ref):
    qi = pl.program_id(2)
    hp, nb = kn_ref.shape[0], kn_ref.shape[1]
    blk = MOBA_BLOCK
    heads = range(hp)

    def cols(h):
        return slice(h * HEAD_DIM, (h + 1) * HEAD_DIM)

    @pl.when(qi == 0)
    def _():
        def prep(n, carry):
            r = pl.multiple_of(n * blk, blk)
            for h in heads:
                kb = k_ref[pl.ds(r, blk), cols(h)].astype(F32)
                ms = jnp.mean(kb * kb, axis=-1, keepdims=True)
                kn = kb * lax.rsqrt(ms + NORM_EPS) * kg_ref[...]
                kn_ref[h, n] = kn.astype(BF16)
                km_ref[h, pl.ds(n, 1), :] = jnp.mean(kn, axis=0, keepdims=True)
                vt_ref[h, n] = v_ref[pl.ds(r, blk), cols(h)].astype(F32).T.astype(BF16)
            return carry

        lax.fori_loop(0, nb, prep, 0)

    q = [q_ref[:, cols(h)].astype(F32) for h in heads]
    qn = [t * lax.rsqrt(jnp.mean(t * t, axis=-1, keepdims=True) + NORM_EPS) * qg_ref[...] for t in q]

    gate = [_dot_nt(km_ref[h], qn[h], precision=lax.Precision.HIGHEST) for h in heads]
    rows = lax.broadcasted_iota(jnp.int32, (nb, blk), 0).astype(F32)
    past = rows < qi.astype(F32)
    gate = [jnp.where(past, g, NEG) for g in gate]
    sel = [jnp.zeros((nb, blk), F32) for _ in heads]
    for _ in range(MOBA_TOPK):
        mx = [jnp.max(g, axis=0, keepdims=True) for g in gate]
        idx = [jnp.min(jnp.where(gate[h] == mx[h], rows, float(nb)), axis=0, keepdims=True) for h in heads]
        pick = [jnp.logical_and(rows == idx[h], mx[h] > 0.5 * NEG) for h in heads]
        sel = [jnp.where(pick[h], 1.0, sel[h]) for h in heads]
        gate = [jnp.where(pick[h], NEG, gate[h]) for h in heads]
    for h in heads:
        sel_ref[h] = sel[h]

    qb = [(t * (HEAD_DIM ** -0.5)).astype(BF16) for t in qn]

    s = [_dot_nt(kn_ref[h, qi], qb[h]) for h in heads]
    kidx = lax.broadcasted_iota(jnp.int32, (blk, blk), 0)
    qidx = lax.broadcasted_iota(jnp.int32, (blk, blk), 1)
    causal = kidx <= qidx
    s = [jnp.where(causal, t, NEG) for t in s]
    m0 = [jnp.max(t, axis=0, keepdims=True) for t in s]
    p = [jnp.exp(s[h] - m0[h]) for h in heads]
    l0 = [jnp.sum(t, axis=0, keepdims=True) for t in p]
    acc0 = [_dot(vt_ref[h, qi], p[h].astype(BF16)) for h in heads]

    kvb = MOBA_KV
    units = [(h, u) for h in heads for u in range(kvb)]

    def body(t, carry):
        m, l, acc = carry
        js = [jnp.minimum(t * kvb + u, nb - 1) for u in range(kvb)]
        sj = {(h, u): _dot_nt(kn_ref[h, js[u]], qb[h]) for h, u in units}
        sj = {(h, u): jnp.where(sel_ref[h, pl.ds(js[u], 1), :] > 0.0, sj[h, u], NEG) for h, u in units}
        bmax = {(h, u): jnp.max(sj[h, u], axis=0, keepdims=True) for h, u in units}
        m_new = []
        for h in heads:
            mh = m[h]
            for u in range(kvb):
                mh = jnp.maximum(mh, bmax[h, u])
            m_new.append(mh)
        alpha = [jnp.exp(m[h] - m_new[h]) for h in heads]
        pj = {(h, u): jnp.exp(sj[h, u] - m_new[h]) for h, u in units}
        psum = {(h, u): jnp.sum(pj[h, u], axis=0, keepdims=True) for h, u in units}
        pv = {(h, u): _dot(vt_ref[h, js[u]], pj[h, u].astype(BF16)) for h, u in units}
        l_new, acc_new = [], []
        for h in heads:
            lh = alpha[h] * l[h]
            ah = alpha[h] * acc[h]
            for u in range(kvb):
                lh = lh + psum[h, u]
                ah = ah + pv[h, u]
            l_new.append(lh)
            acc_new.append(ah)
        return m_new, l_new, acc_new

    _, l, acc = lax.fori_loop(0, (qi + (kvb - 1)) // kvb, body, (m0, l0, acc0))
    for h in heads:
        o_ref[:, cols(h)] = (acc[h] / l[h]).T.astype(o_ref.dtype)


def _moba(proj, q_gain, k_gain, batch, seq):
    n = batch * seq
    nb = seq // MOBA_BLOCK
    hp = MOBA_GROUP
    gw = hp * HEAD_DIM
    groups = MOBA_HEADS // hp
    return pl.pallas_call(
        _moba_kernel, grid=(batch, groups, nb),
        in_specs=[
            pl.BlockSpec((MOBA_BLOCK, gw), lambda b, g, i: (b * nb + i, g)),
            pl.BlockSpec((seq, gw), lambda b, g, i: (b, groups + g)),
            pl.BlockSpec((seq, gw), lambda b, g, i: (b, 2 * groups + g)),
            pl.BlockSpec((1, HEAD_DIM), lambda b, g, i: (0, 0)),
            pl.BlockSpec((1, HEAD_DIM), lambda b, g, i: (0, 0)),
        ],
        out_specs=pl.BlockSpec((MOBA_BLOCK, gw), lambda b, g, i: (b * nb + i, g)),
        out_shape=jax.ShapeDtypeStruct((n, MOBA_WIDTH), BF16),
        scratch_shapes=[
            pltpu.VMEM((hp, nb, MOBA_BLOCK, HEAD_DIM), BF16),
            pltpu.VMEM((hp, nb, HEAD_DIM, MOBA_BLOCK), BF16),
            pltpu.VMEM((hp, nb, HEAD_DIM), F32),
            pltpu.VMEM((hp, nb, MOBA_BLOCK), F32),
        ],
        compiler_params=_params("arbitrary", "arbitrary", "arbitrary"), name="moba",
    )(proj, proj, proj, q_gain, k_gain)


def _dot_split(a16, b):
    hi = b.astype(BF16)
    lo = (b - hi.astype(F32)).astype(BF16)
    return _dot(a16, hi) + _dot(a16, lo)


def _dn_kernel(qkv_ref, z_ref, sm_ref, cw_ref, al_ref, dtb_ref, og_ref, o_ref, xe_ref, st_ref):
    c = DN_TILE
    width = qkv_ref.shape[1]

    @pl.when(pl.program_id(1) == 0)
    def _():
        xe_ref[0:DN_HIST, :] = jnp.zeros((DN_HIST, width), BF16)
        st_ref[...] = jnp.zeros(st_ref.shape, F32)

    xe_ref[DN_HIST:DN_HIST + c, :] = qkv_ref[...]
    xe = xe_ref[...]
    sr = lax.broadcasted_iota(jnp.int32, (c, DN_HIST + c), 0)
    sc = lax.broadcasted_iota(jnp.int32, (c, DN_HIST + c), 1)
    y = cw_ref[DN_CONV - 1:DN_CONV, :] * qkv_ref[...].astype(F32)
    for d in range(1, DN_CONV):
        shift = jnp.where(sc == sr + (DN_HIST - d), 1.0, 0.0).astype(BF16)
        y = y + cw_ref[DN_CONV - 1 - d:DN_CONV - d, :] * _dot(shift, xe)
    xe_ref[0:DN_HIST, :] = xe_ref[c:c + DN_HIST, :]
    y = y * _sigmoid(y)

    sm = sm_ref[...]
    beta_all = _sigmoid(sm)
    xg = sm + dtb_ref[...]
    softplus = jnp.maximum(xg, 0.0) + jnp.log(1.0 + jnp.exp(-jnp.abs(xg)))
    g_all = -jnp.exp(al_ref[...]) * softplus

    ri = lax.broadcasted_iota(jnp.int32, (c, c), 0)
    ci = lax.broadcasted_iota(jnp.int32, (c, c), 1)
    tril = ri >= ci
    strict = ri > ci
    eye = jnp.where(ri == ci, 1.0, 0.0)
    lower_blocks = []
    s = 1
    while s < c:
        same_2s = ((ri ^ ci) & ~(2 * s - 1)) == 0
        lower_blocks.append(jnp.logical_and(same_2s, jnp.logical_and((ri & s) != 0, (ci & s) == 0)))
        s *= 2
    gcum = _dot_split(jnp.where(tril, 1.0, 0.0).astype(BF16), g_all)
    gcum_t = gcum.T

    heads = range(DN_HEADS)

    def head_cols(base, h):
        return y[:, base + h * HEAD_DIM:base + (h + 1) * HEAD_DIM]

    q = [head_cols(0, h) for h in heads]
    k = [head_cols(DN_WIDTH, h) for h in heads]
    v = [head_cols(2 * DN_WIDTH, h) for h in heads]
    q = [t * lax.rsqrt(jnp.sum(t * t, axis=-1, keepdims=True) + NORM_EPS) * (HEAD_DIM ** -0.5) for t in q]
    k = [t * lax.rsqrt(jnp.sum(t * t, axis=-1, keepdims=True) + NORM_EPS) for t in k]
    beta = [beta_all[:, h:h + 1] for h in heads]
    g_col = [gcum[:, DN_HEADS + h:DN_HEADS + h + 1] for h in heads]
    g_row = [gcum_t[DN_HEADS + h:DN_HEADS + h + 1, :] for h in heads]
    g_last = [t[c - 1:c, :] for t in g_col]
    decay = [jnp.where(tril, jnp.exp(jnp.minimum(g_col[h] - g_row[h], 0.0)), 0.0) for h in heads]
    e_g = [jnp.exp(t) for t in g_col]
    kb = [k[h] * beta[h] for h in heads]
    k16 = [t.astype(BF16) for t in k]
    nmat = [jnp.where(strict, _dot_nt(kb[h].astype(BF16), k16[h]) * decay[h], 0.0) for h in heads]
    attn = [(_dot_nt(q[h].astype(BF16), k16[h]) * decay[h]).astype(BF16) for h in heads]

    rhs = [jnp.concatenate([v[h] * beta[h], kb[h] * e_g[h]], axis=1).astype(BF16) for h in heads]
    n16 = [t.astype(BF16) for t in nmat]
    tinv = [eye - jnp.where(lower_blocks[0], t, 0.0) for t in nmat]
    for msk in lower_blocks[1:]:
        t16 = [t.astype(BF16) for t in tinv]
        cm = [jnp.where(msk, t, jnp.zeros_like(t)) for t in n16]
        p = [_dot(cm[h], t16[h]).astype(BF16) for h in heads]
        tinv = [tinv[h] - _dot(t16[h], p[h]) for h in heads]
    sol = [_dot(tinv[h].astype(BF16), rhs[h]) for h in heads]

    state = [st_ref[h] for h in heads]
    s16 = [t.astype(BF16) for t in state]
    v16 = [(sol[h][:, :HEAD_DIM] - _dot(sol[h][:, HEAD_DIM:].astype(BF16), s16[h])).astype(BF16)
           for h in heads]
    o = [_dot((q[h] * e_g[h]).astype(BF16), s16[h]) + _dot(attn[h], v16[h]) for h in heads]
    kd = [(k[h] * jnp.exp(g_last[h] - g_col[h])).astype(BF16) for h in heads]
    for h in heads:
        st_ref[h] = state[h] * jnp.exp(g_last[h]) + _dot_tn(kd[h], v16[h])
    for h in heads:
        on = o[h] * lax.rsqrt(jnp.mean(o[h] * o[h], axis=-1, keepdims=True) + NORM_EPS) * og_ref[...]
        zh = z_ref[:, h * HEAD_DIM:(h + 1) * HEAD_DIM].astype(F32)
        o_ref[:, h * HEAD_DIM:(h + 1) * HEAD_DIM] = (on * (zh * _sigmoid(zh))).astype(o_ref.dtype)


def _deltanet(proj, small, conv_w, a_log_row, dt_bias_row, out_gain, batch, seq):
    n = batch * seq
    nt = seq // DN_TILE
    qkv_w = 3 * DN_WIDTH
    qkv_blk = (3 * MOBA_WIDTH) // qkv_w
    z_blk = (3 * MOBA_WIDTH + qkv_w) // DN_WIDTH
    return pl.pallas_call(
        _dn_kernel, grid=(batch, nt),
        in_specs=[
            pl.BlockSpec((DN_TILE, qkv_w), lambda b, t: (b * nt + t, qkv_blk)),
            pl.BlockSpec((DN_TILE, DN_WIDTH), lambda b, t: (b * nt + t, z_blk)),
            pl.BlockSpec((DN_TILE, SMALL_COLS), lambda b, t: (b * nt + t, 0)),
            pl.BlockSpec((8, qkv_w), lambda b, t: (0, 0)),
            pl.BlockSpec((1, SMALL_COLS), lambda b, t: (0, 0)),
            pl.BlockSpec((1, SMALL_COLS), lambda b, t: (0, 0)),
            pl.BlockSpec((1, HEAD_DIM), lambda b, t: (0, 0)),
        ],
        out_specs=pl.BlockSpec((DN_TILE, DN_WIDTH), lambda b, t: (b * nt + t, 0)),
        out_shape=jax.ShapeDtypeStruct((n, DN_WIDTH), BF16),
        scratch_shapes=[
            pltpu.VMEM((DN_HIST + DN_TILE, qkv_w), BF16),
            pltpu.VMEM((DN_HEADS, HEAD_DIM, HEAD_DIM), F32),
        ],
        compiler_params=_params("arbitrary", "arbitrary"), name="deltanet",
    )(proj, proj, small, conv_w, a_log_row, dt_bias_row, out_gain)


def _merge_kernel(ya_ref, yb_ref, wa_ref, wb_ref, ga_ref, gb_ref, o_ref):
    a = _dot(ya_ref[...], wa_ref[...])
    b = _dot(yb_ref[...], wb_ref[...])
    ga = _sigmoid(ga_ref[...].astype(F32))
    gb = _sigmoid(gb_ref[...].astype(F32))
    o_ref[...] = (ga * a + gb * b).astype(o_ref.dtype)


def _merge(y_a, y_b, w_a, w_b, proj):
    n = y_a.shape[0]
    tm = min(MM_TM, n)
    ga_blk = (3 * MOBA_WIDTH + 4 * DN_WIDTH) // MM_TN
    gb_blk = ga_blk + D_MODEL // MM_TN
    return pl.pallas_call(
        _merge_kernel, grid=(n // tm, D_MODEL // MM_TN),
        in_specs=[
            pl.BlockSpec((tm, MOBA_WIDTH), lambda i, j: (i, 0)),
            pl.BlockSpec((tm, DN_WIDTH), lambda i, j: (i, 0)),
            pl.BlockSpec((MOBA_WIDTH, MM_TN), lambda i, j: (0, j)),
            pl.BlockSpec((DN_WIDTH, MM_TN), lambda i, j: (0, j)),
            pl.BlockSpec((tm, MM_TN), lambda i, j: (i, ga_blk + j)),
            pl.BlockSpec((tm, MM_TN), lambda i, j: (i, gb_blk + j)),
        ],
        out_specs=pl.BlockSpec((tm, MM_TN), lambda i, j: (i, j)),
        out_shape=jax.ShapeDtypeStruct((n, D_MODEL), BF16),
        compiler_params=_params("arbitrary", "arbitrary"), name="merge",
    )(y_a, y_b, w_a, w_b, proj, proj)


def _res_mm_kernel(a_ref, w_ref, x_ref, o_ref):
    o_ref[...] = x_ref[...] + _dot(a_ref[...], w_ref[...])


def _res_mm(a, w, x2, tn):
    n, k = a.shape
    cols = w.shape[1]
    tm = min(MM_TM, n)
    return pl.pallas_call(
        _res_mm_kernel, grid=(n // tm, cols // tn),
        in_specs=[
            pl.BlockSpec((tm, k), lambda i, j: (i, 0)),
            pl.BlockSpec((k, tn), lambda i, j: (0, j)),
            pl.BlockSpec((tm, tn), lambda i, j: (i, j)),
        ],
        out_specs=pl.BlockSpec((tm, tn), lambda i, j: (i, j)),
        out_shape=jax.ShapeDtypeStruct((n, cols), F32),
        compiler_params=_params("arbitrary", "arbitrary"), name="res_mm",
    )(a, w, x2)


def _ffn_in_kernel(tiles_per_seq, x_ref, g_ref, wg_ref, wu_ref, cw_ref, cb_ref, o_ref,
                   h_ref, cs_ref, tail_ref):
    i = pl.program_id(0)
    j = pl.program_id(1)
    tm = x_ref.shape[0]

    @pl.when(j == 0)
    def _():
        _rms_rows(x_ref, g_ref, h_ref)

    seq_start = (i % tiles_per_seq) == 0

    @pl.when(seq_start)
    def _():
        cs_ref[0:8, :] = jnp.zeros((8, cs_ref.shape[1]), F32)

    @pl.when(jnp.logical_not(seq_start))
    def _():
        cs_ref[0:8, :] = tail_ref[j]

    rc = min(FFN_ROWS, tm)

    def dots(r):
        h = h_ref[r * rc:(r + 1) * rc, :]
        cs_ref[8 + r * rc:8 + (r + 1) * rc, :] = _dot(h, wg_ref[...])
        return _dot(h, wu_ref[...])

    def epilogue(r, up):
        y = cb_ref[...] + cw_ref[0:1, :] * cs_ref[6 + r * rc:6 + (r + 1) * rc, :]
        for t in range(1, FFN_CONV):
            y = y + cw_ref[t:t + 1, :] * cs_ref[6 + t + r * rc:6 + t + (r + 1) * rc, :]
        o_ref[r * rc:(r + 1) * rc, :] = (y * _sigmoid(y) * up).astype(o_ref.dtype)

    up_prev = dots(0)
    for r in range(1, tm // rc):
        up_next = dots(r)
        epilogue(r - 1, up_prev)
        up_prev = up_next
    tail_ref[j] = cs_ref[tm:tm + 8, :]
    epilogue(tm // rc - 1, up_prev)


def _ffn_in(x2, gain, w_in16, conv_w, conv_b, seq):
    n, d = x2.shape
    tm = min(MM_TM, seq)
    nj = D_FF // FFN_TN
    return pl.pallas_call(
        functools.partial(_ffn_in_kernel, seq // tm),
        grid=(n // tm, nj),
        in_specs=[
            pl.BlockSpec((tm, d), lambda i, j: (i, 0)),
            pl.BlockSpec((1, d), lambda i, j: (0, 0)),
            pl.BlockSpec((d, FFN_TN), lambda i, j: (0, j)),
            pl.BlockSpec((d, FFN_TN), lambda i, j: (0, nj + j)),
            pl.BlockSpec((8, FFN_TN), lambda i, j: (0, j)),
            pl.BlockSpec((1, FFN_TN), lambda i, j: (0, j)),
        ],
        out_specs=pl.BlockSpec((tm, FFN_TN), lambda i, j: (i, j)),
        out_shape=jax.ShapeDtypeStruct((n, D_FF), BF16),
        scratch_shapes=[
            pltpu.VMEM((tm, d), BF16),
            pltpu.VMEM((tm + 8, FFN_TN), F32),
            pltpu.VMEM((nj, 8, FFN_TN), F32),
        ],
        compiler_params=_params("arbitrary", "arbitrary"), name="ffn_in",
    )(x2, gain, w_in16, w_in16, conv_w, conv_b)


def _ffn(x2, gain, w_ffn_in, conv_w, conv_b, w_down, seq):
    act = _ffn_in(x2, gain[None, :], w_ffn_in.astype(BF16), _pad_rows(conv_w, 8), conv_b[None, :], seq)
    return _res_mm(act, w_down.astype(BF16), x2, FFN_TN)


def _pad_rows(w, rows):
    return jnp.pad(w, ((0, rows - w.shape[0]), (0, 0)))


def _lane_row(v, offset):
    return jnp.zeros((1, SMALL_COLS), F32).at[0, offset:offset + v.shape[0]].set(v.astype(F32))


def kernel(x, attn_norm, w_in, moba_q_norm, moba_k_norm, dn_conv, dn_a_log, dn_dt_bias, dn_out_norm,
           w_branch_a, w_branch_b, w_out, ffn_norm, w_ffn_in, ffn_conv, ffn_conv_bias, w_ffn_down):
    batch, seq, d = x.shape
    depth = w_in.shape[0]
    n = batch * seq
    assert d == D_MODEL and seq % MOBA_BLOCK == 0 and seq % DN_TILE == 0 and n % min(MM_TM, n) == 0
    main_w = 3 * MOBA_WIDTH + 4 * DN_WIDTH
    small_w = 2 * DN_HEADS
    x2 = x.reshape(n, d)
    for l in range(depth):
        w_main = jnp.concatenate([w_in[l, :, :main_w], w_in[l, :, main_w + small_w:]], axis=1).astype(BF16)
        w_small = jnp.pad(w_in[l, :, main_w:main_w + small_w],
                          ((0, 0), (0, SMALL_COLS - small_w))).astype(BF16)
        proj, small = _norm_proj(x2, attn_norm[l][None, :], w_main, w_small)
        y_a = _moba(proj, moba_q_norm[l][None, :], moba_k_norm[l][None, :], batch, seq)
        y_b = _deltanet(proj, small, _pad_rows(dn_conv[l], 8), _lane_row(dn_a_log[l], DN_HEADS),
                        _lane_row(dn_dt_bias[l], DN_HEADS), dn_out_norm[l][None, :], batch, seq)
        merged = _merge(y_a, y_b, w_branch_a[l].astype(BF16), w_branch_b[l].astype(BF16), proj)
        x2 = _res_mm(merged, w_out[l].astype(BF16), x2, MM_TN)
        x2 = _ffn(x2, ffn_norm[l], w_ffn_in[l], ffn_conv[l], ffn_conv_bias[l], w_ffn_down[l], seq)
    return x2.reshape(batch, seq, d)
```

```python
import functools

import jax
import jax.numpy as jnp
from jax import lax
from jax.experimental import pallas as pl
from jax.experimental.pallas import tpu as pltpu

F32 = jnp.float32
BF16 = jnp.bfloat16

D_MODEL = 2048
HEAD_DIM = 128
MOBA_HEADS = 8
MOBA_WIDTH = MOBA_HEADS * HEAD_DIM
MOBA_BLOCK = 256
MOBA_TOPK = 3
DN_HEADS = 8
DN_WIDTH = DN_HEADS * HEAD_DIM
DN_CONV = 4
D_FF = 5632
FFN_CONV = 3
NORM_EPS = 1e-6

MAIN_COLS = 3 * MOBA_WIDTH + 4 * DN_WIDTH
PROJ_COLS = MAIN_COLS + 2 * D_MODEL
SMALL_COLS = 128

VMEM_LIMIT_BYTES = 56 * 1024 * 1024

MM_TM = 1024
MM_TN = 1024
NORM_ROWS = 128
MOBA_GROUP = 8
MOBA_KV = 2
MOBA_VROWS = HEAD_DIM + 16
LOG2E = 1.4426950408889634
DN_TILE = 128
DN_SEQS = 2
DN_HIST = 16
FFN_TN = 512
FFN_ROWS = 256
NEG = -1e30


def _sigmoid(x):
    return 1.0 / (1.0 + jnp.exp(-x))


def _dot(a, b):
    return jnp.dot(a, b, preferred_element_type=F32)


def _dot_nt(a, b, precision=None):
    return lax.dot_general(a, b, (((1,), (1,)), ((), ())), precision=precision,
                           preferred_element_type=F32)


def _dot_tn(a, b):
    return lax.dot_general(a, b, (((0,), (0,)), ((), ())), preferred_element_type=F32)


def _params(*sem):
    return pltpu.CompilerParams(dimension_semantics=sem, vmem_limit_bytes=VMEM_LIMIT_BYTES)


def _rms_rows(x_ref, g_ref, h_ref):
    rows = x_ref.shape[0]

    def body(c, carry):
        r = pl.multiple_of(c * NORM_ROWS, NORM_ROWS)
        x = x_ref[pl.ds(r, NORM_ROWS), :]
        ms = jnp.mean(x * x, axis=-1, keepdims=True)
        h_ref[pl.ds(r, NORM_ROWS), :] = (x * lax.rsqrt(ms + NORM_EPS) * g_ref[...]).astype(h_ref.dtype)
        return carry

    lax.fori_loop(0, rows // NORM_ROWS, body, 0)


def _in_proj_kernel(n_main, x_ref, g_ref, wm_ref, wg_ref, ws_ref, o_ref, s_ref, h_ref):
    j = pl.program_id(1)

    @pl.when(j == 0)
    def _():
        _rms_rows(x_ref, g_ref, h_ref)
        s_ref[...] = _dot(h_ref[...], ws_ref[...])

    @pl.when(j < n_main)
    def _():
        o_ref[...] = _dot(h_ref[...], wm_ref[...]).astype(o_ref.dtype)

    @pl.when(j >= n_main)
    def _():
        o_ref[...] = _dot(h_ref[...], wg_ref[...]).astype(o_ref.dtype)


def _in_proj(x2, gain, w_in16, w_gates16, w_small16, layer):
    n, d = x2.shape
    tm = min(MM_TM, n)
    n_main = MAIN_COLS // MM_TN
    n_gate = w_gates16.shape[2] // MM_TN
    return pl.pallas_call(
        functools.partial(_in_proj_kernel, n_main),
        grid=(n // tm, n_main + n_gate),
        in_specs=[
            pl.BlockSpec((tm, d), lambda i, j: (i, 0)),
            pl.BlockSpec((1, d), lambda i, j: (0, 0)),
            pl.BlockSpec((None, d, MM_TN), lambda i, j: (layer, 0, jnp.minimum(j, n_main - 1))),
            pl.BlockSpec((None, d, MM_TN), lambda i, j: (layer, 0, jnp.maximum(j - n_main, 0))),
            pl.BlockSpec((None, d, SMALL_COLS), lambda i, j: (layer, 0, 0)),
        ],
        out_specs=[pl.BlockSpec((tm, MM_TN), lambda i, j: (i, j)),
                   pl.BlockSpec((tm, SMALL_COLS), lambda i, j: (i, 0))],
        out_shape=[jax.ShapeDtypeStruct((n, PROJ_COLS), BF16),
                   jax.ShapeDtypeStruct((n, SMALL_COLS), F32)],
        scratch_shapes=[pltpu.VMEM((tm, d), BF16)],
        compiler_params=_params("arbitrary", "arbitrary"), name="in_proj",
    )(x2, gain, w_in16, w_gates16, w_small16)


def _moba_kernel(q_ref, k_ref, v_ref, qg_ref, kg_ref, o_ref, kn_ref, vt_ref, km_ref, sel_ref):
    qi = pl.program_id(2)
    hp, nb = kn_ref.shape[0], kn_ref.shape[1]
    blk = MOBA_BLOCK
    heads = range(hp)

    def cols(h):
        return slice(h * HEAD_DIM, (h + 1) * HEAD_DIM)

    @pl.when(qi == 0)
    def _():
        tail_rows = lax.broadcasted_iota(jnp.int32, (MOBA_VROWS - HEAD_DIM, blk), 0)
        ones_tail = jnp.where(tail_rows == 0, 1.0, 0.0).astype(BF16)

        def prep(n, carry):
            r = pl.multiple_of(n * blk, blk)
            for h in heads:
                kb = k_ref[pl.ds(r, blk), cols(h)].astype(F32)
                ms = jnp.mean(kb * kb, axis=-1, keepdims=True)
                kn = kb * lax.rsqrt(ms + NORM_EPS) * kg_ref[...]
                kn_ref[h, n] = kn.astype(BF16)
                km_ref[h, pl.ds(n, 1), :] = jnp.mean(kn, axis=0, keepdims=True)
                vt_ref[h, n, 0:HEAD_DIM, :] = v_ref[pl.ds(r, blk), cols(h)].astype(F32).T.astype(BF16)
                vt_ref[h, n, HEAD_DIM:MOBA_VROWS, :] = ones_tail
            return carry

        lax.fori_loop(0, nb, prep, 0)

    q = [q_ref[:, cols(h)].astype(F32) for h in heads]
    qn = [t * lax.rsqrt(jnp.mean(t * t, axis=-1, keepdims=True) + NORM_EPS) * qg_ref[...] for t in q]

    gate = [_dot_nt(km_ref[h], qn[h], precision=lax.Precision.HIGHEST) for h in heads]
    rows = lax.broadcasted_iota(jnp.int32, (nb, blk), 0).astype(F32)
    past = rows < qi.astype(F32)
    gate = [jnp.where(past, g, NEG) for g in gate]
    sel = [jnp.zeros((nb, blk), F32) for _ in heads]
    for _ in range(MOBA_TOPK):
        mx = [jnp.max(g, axis=0, keepdims=True) for g in gate]
        idx = [jnp.min(jnp.where(gate[h] == mx[h], rows, float(nb)), axis=0, keepdims=True) for h in heads]
        pick = [jnp.logical_and(rows == idx[h], mx[h] > 0.5 * NEG) for h in heads]
        sel = [jnp.where(pick[h], 1.0, sel[h]) for h in heads]
        gate = [jnp.where(pick[h], NEG, gate[h]) for h in heads]
    for h in heads:
        sel_ref[h] = sel[h]

    qb = [(t * (HEAD_DIM ** -0.5 * LOG2E)).astype(BF16) for t in qn]

    s = [_dot_nt(kn_ref[h, qi], qb[h]) for h in heads]
    kidx = lax.broadcasted_iota(jnp.int32, (blk, blk), 0)
    qidx = lax.broadcasted_iota(jnp.int32, (blk, blk), 1)
    causal = kidx <= qidx
    s = [jnp.where(causal, t, NEG) for t in s]
    m0 = [jnp.max(t, axis=0, keepdims=True) for t in s]
    p = [jnp.exp2(s[h] - m0[h]) for h in heads]
    acc0 = [_dot(vt_ref[h, qi], p[h].astype(BF16)) for h in heads]

    kvb = MOBA_KV
    units = [(h, u) for h in heads for u in range(kvb)]

    def body(t, carry):
        m, acc = carry
        js = [jnp.minimum(t * kvb + u, nb - 1) for u in range(kvb)]
        sj = {(h, u): _dot_nt(kn_ref[h, js[u]], qb[h]) for h, u in units}
        sj = {(h, u): jnp.where(sel_ref[h, pl.ds(js[u], 1), :] > 0.0, sj[h, u], NEG) for h, u in units}
        bmax = {(h, u): jnp.max(sj[h, u], axis=0, keepdims=True) for h, u in units}
        m_new = []
        for h in heads:
            mh = m[h]
            for u in range(kvb):
                mh = jnp.maximum(mh, bmax[h, u])
            m_new.append(mh)
        alpha = [jnp.exp2(m[h] - m_new[h]) for h in heads]
        pj = {(h, u): jnp.exp2(sj[h, u] - m_new[h]).astype(BF16) for h, u in units}
        pv = {(h, u): _dot(vt_ref[h, js[u]], pj[h, u]) for h, u in units}
        acc_new = []
        for h in heads:
            ah = alpha[h] * acc[h]
            for u in range(kvb):
                ah = ah + pv[h, u]
            acc_new.append(ah)
        return m_new, acc_new

    _, acc = lax.fori_loop(0, (qi + (kvb - 1)) // kvb, body, (m0, acc0))
    for h in heads:
        out = acc[h][0:HEAD_DIM, :] / acc[h][HEAD_DIM:HEAD_DIM + 1, :]
        o_ref[:, cols(h)] = out.T.astype(o_ref.dtype)


def _moba(proj, q_gain, k_gain, batch, seq):
    n = batch * seq
    nb = seq // MOBA_BLOCK
    hp = MOBA_GROUP
    gw = hp * HEAD_DIM
    groups = MOBA_HEADS // hp
    return pl.pallas_call(
        _moba_kernel, grid=(batch, groups, nb),
        in_specs=[
            pl.BlockSpec((MOBA_BLOCK, gw), lambda b, g, i: (b * nb + i, g)),
            pl.BlockSpec((seq, gw), lambda b, g, i: (b, groups + g), pipeline_mode=pl.Buffered(1)),
            pl.BlockSpec((seq, gw), lambda b, g, i: (b, 2 * groups + g), pipeline_mode=pl.Buffered(1)),
            pl.BlockSpec((1, HEAD_DIM), lambda b, g, i: (0, 0)),
            pl.BlockSpec((1, HEAD_DIM), lambda b, g, i: (0, 0)),
        ],
        out_specs=pl.BlockSpec((MOBA_BLOCK, gw), lambda b, g, i: (b * nb + i, g)),
        out_shape=jax.ShapeDtypeStruct((n, MOBA_WIDTH), BF16),
        scratch_shapes=[
            pltpu.VMEM((hp, nb, MOBA_BLOCK, HEAD_DIM), BF16),
            pltpu.VMEM((hp, nb, MOBA_VROWS, MOBA_BLOCK), BF16),
            pltpu.VMEM((hp, nb, HEAD_DIM), F32),
            pltpu.VMEM((hp, nb, MOBA_BLOCK), F32),
        ],
        compiler_params=_params("arbitrary", "arbitrary", "arbitrary"), name="moba",
    )(proj, proj, proj, q_gain, k_gain)


def _dot_split(a16, b):
    hi = b.astype(BF16)
    lo = (b - hi.astype(F32)).astype(BF16)
    return _dot(a16, hi) + _dot(a16, lo)


def _dn_kernel(qkv_ref, z_ref, sm_ref, cw_ref, al_ref, dtb_ref, og_ref, o_ref, xe_ref, st_ref):
    c = DN_TILE
    ns, _, width = qkv_ref.shape
    seqs = range(ns)

    @pl.when(pl.program_id(1) == 0)
    def _():
        xe_ref[:, 0:DN_HIST, :] = jnp.zeros((ns, DN_HIST, width), BF16)
        st_ref[...] = jnp.zeros(st_ref.shape, F32)

    ri = lax.broadcasted_iota(jnp.int32, (c, c), 0)
    ci = lax.broadcasted_iota(jnp.int32, (c, c), 1)
    tril = ri >= ci
    strict = ri > ci
    eye = jnp.where(ri == ci, 1.0, 0.0)
    lower_blocks = []
    s = 1
    while s < c:
        same_2s = ((ri ^ ci) & ~(2 * s - 1)) == 0
        lower_blocks.append(jnp.logical_and(same_2s, jnp.logical_and((ri & s) != 0, (ci & s) == 0)))
        s *= 2
    tril16 = jnp.where(tril, 1.0, 0.0).astype(BF16)
    sr = lax.broadcasted_iota(jnp.int32, (c, DN_HIST + c), 0)
    sc = lax.broadcasted_iota(jnp.int32, (c, DN_HIST + c), 1)
    shifts = [jnp.where(sc == sr + (DN_HIST - d), 1.0, 0.0).astype(BF16) for d in range(1, DN_CONV)]


    ys = []
    for b in seqs:
        xe_ref[b, DN_HIST:DN_HIST + c, :] = qkv_ref[b]
        xe = xe_ref[b]
        y = cw_ref[DN_CONV - 1:DN_CONV, :] * qkv_ref[b].astype(F32)
        for d in range(1, DN_CONV):
            y = y + cw_ref[DN_CONV - 1 - d:DN_CONV - d, :] * _dot(shifts[d - 1], xe)
        xe_ref[b, 0:DN_HIST, :] = xe_ref[b, c:c + DN_HIST, :]
        ys.append(y * _sigmoid(y))

    sm = [sm_ref[b] for b in seqs]
    beta_all = [_sigmoid(t) for t in sm]
    xg = [t + dtb_ref[...] for t in sm]
    softplus = [jnp.maximum(t, 0.0) + jnp.log(1.0 + jnp.exp(-jnp.abs(t))) for t in xg]
    g_all = [-jnp.exp(al_ref[...]) * t for t in softplus]
    gcum = [_dot_split(tril16, t) for t in g_all]
    gcum_t = [t.T for t in gcum]

    units = [(b, h) for b in seqs for h in range(DN_HEADS)]

    def head_cols(b, base, h):
        return ys[b][:, base + h * HEAD_DIM:base + (h + 1) * HEAD_DIM]

    q = {u: head_cols(u[0], 0, u[1]) for u in units}
    k = {u: head_cols(u[0], DN_WIDTH, u[1]) for u in units}
    v = {u: head_cols(u[0], 2 * DN_WIDTH, u[1]) for u in units}
    q = {u: t * lax.rsqrt(jnp.sum(t * t, axis=-1, keepdims=True) + NORM_EPS) * (HEAD_DIM ** -0.5)
         for u, t in q.items()}
    k = {u: t * lax.rsqrt(jnp.sum(t * t, axis=-1, keepdims=True) + NORM_EPS) for u, t in k.items()}
    beta = {(b, h): beta_all[b][:, h:h + 1] for b, h in units}
    g_col = {(b, h): gcum[b][:, DN_HEADS + h:DN_HEADS + h + 1] for b, h in units}
    g_row = {(b, h): gcum_t[b][DN_HEADS + h:DN_HEADS + h + 1, :] for b, h in units}
    g_last = {u: t[c - 1:c, :] for u, t in g_col.items()}
    decay = {u: jnp.where(tril, jnp.exp(jnp.minimum(g_col[u] - g_row[u], 0.0)), 0.0) for u in units}
    e_g = {u: jnp.exp(t) for u, t in g_col.items()}
    kb = {u: k[u] * beta[u] for u in units}
    k16 = {u: t.astype(BF16) for u, t in k.items()}
    nmat = {u: jnp.where(strict, _dot_nt(kb[u].astype(BF16), k16[u]) * decay[u], 0.0) for u in units}
    attn = {u: (_dot_nt(q[u].astype(BF16), k16[u]) * decay[u]).astype(BF16) for u in units}

    rhs = {u: jnp.concatenate([v[u] * beta[u], kb[u] * e_g[u]], axis=1).astype(BF16) for u in units}
    n16 = {u: t.astype(BF16) for u, t in nmat.items()}
    tinv = {u: eye - jnp.where(lower_blocks[0], t, 0.0) for u, t in nmat.items()}
    for msk in lower_blocks[1:]:
        t16 = {u: t.astype(BF16) for u, t in tinv.items()}
        cm = {u: jnp.where(msk, t, jnp.zeros_like(t)) for u, t in n16.items()}
        p = {u: _dot(cm[u], t16[u]).astype(BF16) for u in units}
        tinv = {u: tinv[u] - _dot(t16[u], p[u]) for u in units}
    sol = {u: _dot(tinv[u].astype(BF16), rhs[u]) for u in units}

    state = {(b, h): st_ref[b, h] for b, h in units}
    s16 = {u: t.astype(BF16) for u, t in state.items()}
    v16 = {u: (sol[u][:, :HEAD_DIM] - _dot(sol[u][:, HEAD_DIM:].astype(BF16), s16[u])).astype(BF16)
           for u in units}
    o = {u: _dot((q[u] * e_g[u]).astype(BF16), s16[u]) + _dot(attn[u], v16[u]) for u in units}
    kd = {u: (k[u] * jnp.exp(g_last[u] - g_col[u])).astype(BF16) for u in units}
    for b, h in units:
        st_ref[b, h] = state[b, h] * jnp.exp(g_last[b, h]) + _dot_tn(kd[b, h], v16[b, h])
    for b, h in units:
        ou = o[b, h]
        on = ou * lax.rsqrt(jnp.mean(ou * ou, axis=-1, keepdims=True) + NORM_EPS) * og_ref[...]
        zh = z_ref[b, :, h * HEAD_DIM:(h + 1) * HEAD_DIM].astype(F32)
        o_ref[b, :, h * HEAD_DIM:(h + 1) * HEAD_DIM] = (on * (zh * _sigmoid(zh))).astype(o_ref.dtype)


def _deltanet(proj, small, conv_w, a_log_row, dt_bias_row, out_gain, batch, seq):
    nt = seq // DN_TILE
    ns = DN_SEQS if batch % DN_SEQS == 0 else 1
    qkv_w = 3 * DN_WIDTH
    qkv_blk = (3 * MOBA_WIDTH) // qkv_w
    z_blk = (3 * MOBA_WIDTH + qkv_w) // DN_WIDTH
    proj3 = proj.reshape(batch, seq, proj.shape[1])
    small3 = small.reshape(batch, seq, SMALL_COLS)
    y_b = pl.pallas_call(
        _dn_kernel, grid=(batch // ns, nt),
        in_specs=[
            pl.BlockSpec((ns, DN_TILE, qkv_w), lambda g, t: (g, t, qkv_blk)),
            pl.BlockSpec((ns, DN_TILE, DN_WIDTH), lambda g, t: (g, t, z_blk)),
            pl.BlockSpec((ns, DN_TILE, SMALL_COLS), lambda g, t: (g, t, 0)),
            pl.BlockSpec((8, qkv_w), lambda g, t: (0, 0)),
            pl.BlockSpec((1, SMALL_COLS), lambda g, t: (0, 0)),
            pl.BlockSpec((1, SMALL_COLS), lambda g, t: (0, 0)),
            pl.BlockSpec((1, HEAD_DIM), lambda g, t: (0, 0)),
        ],
        out_specs=pl.BlockSpec((ns, DN_TILE, DN_WIDTH), lambda g, t: (g, t, 0)),
        out_shape=jax.ShapeDtypeStruct((batch, seq, DN_WIDTH), BF16),
        scratch_shapes=[
            pltpu.VMEM((ns, DN_HIST + DN_TILE, qkv_w), BF16),
            pltpu.VMEM((ns, DN_HEADS, HEAD_DIM, HEAD_DIM), F32),
        ],
        compiler_params=_params("arbitrary", "arbitrary"), name="deltanet",
    )(proj3, proj3, small3, conv_w, a_log_row, dt_bias_row, out_gain)
    return y_b.reshape(batch * seq, DN_WIDTH)


def _merge_kernel(ya_ref, yb_ref, wa_ref, wb_ref, ga_ref, gb_ref, o_ref):
    a = _dot(ya_ref[...], wa_ref[...])
    b = _dot(yb_ref[...], wb_ref[...])
    ga = _sigmoid(ga_ref[...].astype(F32))
    gb = _sigmoid(gb_ref[...].astype(F32))
    o_ref[...] = (ga * a + gb * b).astype(o_ref.dtype)


def _merge(y_a, y_b, w_a, w_b, proj, layer):
    n = y_a.shape[0]
    tm = min(MM_TM, n)
    ga_blk = MAIN_COLS // MM_TN
    gb_blk = ga_blk + D_MODEL // MM_TN
    return pl.pallas_call(
        _merge_kernel, grid=(n // tm, D_MODEL // MM_TN),
        in_specs=[
            pl.BlockSpec((tm, MOBA_WIDTH), lambda i, j: (i, 0)),
            pl.BlockSpec((tm, DN_WIDTH), lambda i, j: (i, 0)),
            pl.BlockSpec((None, MOBA_WIDTH, MM_TN), lambda i, j: (layer, 0, j)),
            pl.BlockSpec((None, DN_WIDTH, MM_TN), lambda i, j: (layer, 0, j)),
            pl.BlockSpec((tm, MM_TN), lambda i, j: (i, ga_blk + j)),
            pl.BlockSpec((tm, MM_TN), lambda i, j: (i, gb_blk + j)),
        ],
        out_specs=pl.BlockSpec((tm, MM_TN), lambda i, j: (i, j)),
        out_shape=jax.ShapeDtypeStruct((n, D_MODEL), BF16),
        compiler_params=_params("arbitrary", "arbitrary"), name="merge",
    )(y_a, y_b, w_a, w_b, proj, proj)


def _res_mm_kernel(a_ref, w_ref, x_ref, o_ref):
    o_ref[...] = x_ref[...] + _dot(a_ref[...], w_ref[...])


def _res_mm(a, w, layer, x2, tn):
    n, k = a.shape
    cols = w.shape[2]
    tm = min(MM_TM, n)
    return pl.pallas_call(
        _res_mm_kernel, grid=(n // tm, cols // tn),
        in_specs=[
            pl.BlockSpec((tm, k), lambda i, j: (i, 0)),
            pl.BlockSpec((None, k, tn), lambda i, j: (layer, 0, j)),
            pl.BlockSpec((tm, tn), lambda i, j: (i, j)),
        ],
        out_specs=pl.BlockSpec((tm, tn), lambda i, j: (i, j)),
        out_shape=jax.ShapeDtypeStruct((n, cols), F32),
        compiler_params=_params("arbitrary", "arbitrary"), name="res_mm",
    )(a, w, x2)


def _ffn_in_kernel(tiles_per_seq, x_ref, g_ref, wg_ref, wu_ref, cw_ref, cb_ref, o_ref,
                   h_ref, cs_ref, tail_ref):
    i = pl.program_id(0)
    j = pl.program_id(1)
    tm = x_ref.shape[0]

    @pl.when(j == 0)
    def _():
        _rms_rows(x_ref, g_ref, h_ref)

    seq_start = (i % tiles_per_seq) == 0

    @pl.when(seq_start)
    def _():
        cs_ref[0:8, :] = jnp.zeros((8, cs_ref.shape[1]), F32)

    @pl.when(jnp.logical_not(seq_start))
    def _():
        cs_ref[0:8, :] = tail_ref[j]

    rc = min(FFN_ROWS, tm)

    def dots(r):
        h = h_ref[r * rc:(r + 1) * rc, :]
        cs_ref[8 + r * rc:8 + (r + 1) * rc, :] = _dot(h, wg_ref[...])
        return _dot(h, wu_ref[...])

    def epilogue(r, up):
        y = cb_ref[...] + cw_ref[0:1, :] * cs_ref[6 + r * rc:6 + (r + 1) * rc, :]
        for t in range(1, FFN_CONV):
            y = y + cw_ref[t:t + 1, :] * cs_ref[6 + t + r * rc:6 + t + (r + 1) * rc, :]
        o_ref[r * rc:(r + 1) * rc, :] = (y * _sigmoid(y) * up).astype(o_ref.dtype)

    up_prev = dots(0)
    for r in range(1, tm // rc):
        up_next = dots(r)
        epilogue(r - 1, up_prev)
        up_prev = up_next
    tail_ref[j] = cs_ref[tm:tm + 8, :]
    epilogue(tm // rc - 1, up_prev)


def _ffn_in(x2, gain, w_in16, layer, conv_w, conv_b, seq):
    n, d = x2.shape
    tm = min(MM_TM, seq)
    nj = D_FF // FFN_TN
    return pl.pallas_call(
        functools.partial(_ffn_in_kernel, seq // tm),
        grid=(n // tm, nj),
        in_specs=[
            pl.BlockSpec((tm, d), lambda i, j: (i, 0)),
            pl.BlockSpec((1, d), lambda i, j: (0, 0)),
            pl.BlockSpec((None, d, FFN_TN), lambda i, j: (layer, 0, j)),
            pl.BlockSpec((None, d, FFN_TN), lambda i, j: (layer, 0, nj + j)),
            pl.BlockSpec((8, FFN_TN), lambda i, j: (0, j)),
            pl.BlockSpec((1, FFN_TN), lambda i, j: (0, j)),
        ],
        out_specs=pl.BlockSpec((tm, FFN_TN), lambda i, j: (i, j)),
        out_shape=jax.ShapeDtypeStruct((n, D_FF), BF16),
        scratch_shapes=[
            pltpu.VMEM((tm, d), BF16),
            pltpu.VMEM((tm + 8, FFN_TN), F32),
            pltpu.VMEM((nj, 8, FFN_TN), F32),
        ],
        compiler_params=_params("arbitrary", "arbitrary"), name="ffn_in",
    )(x2, gain, w_in16, w_in16, conv_w, conv_b)


def _ffn(x2, gain, w_ffn_in16, conv_w, conv_b, w_down16, layer, seq):
    act = _ffn_in(x2, gain[None, :], w_ffn_in16, layer, _pad_rows(conv_w, 8), conv_b[None, :], seq)
    return _res_mm(act, w_down16, layer, x2, FFN_TN)


def _pad_rows(w, rows):
    return jnp.pad(w, ((0, rows - w.shape[0]), (0, 0)))


def _lane_row(v, offset):
    return jnp.zeros((1, SMALL_COLS), F32).at[0, offset:offset + v.shape[0]].set(v.astype(F32))


def kernel(x, attn_norm, w_in, moba_q_norm, moba_k_norm, dn_conv, dn_a_log, dn_dt_bias, dn_out_norm,
           w_branch_a, w_branch_b, w_out, ffn_norm, w_ffn_in, ffn_conv, ffn_conv_bias, w_ffn_down):
    batch, seq, d = x.shape
    depth = w_in.shape[0]
    n = batch * seq
    assert d == D_MODEL and seq % MOBA_BLOCK == 0 and seq % DN_TILE == 0 and n % min(MM_TM, n) == 0
    small_w = 2 * DN_HEADS
    x2 = x.reshape(n, d)
    w_in16 = w_in.astype(BF16)
    w_gates16 = w_in16[:, :, MAIN_COLS + small_w:]
    w_small16 = jnp.pad(w_in16[:, :, MAIN_COLS:MAIN_COLS + small_w], ((0, 0), (0, 0), (0, SMALL_COLS - small_w)))
    w_a16 = w_branch_a.astype(BF16)
    w_b16 = w_branch_b.astype(BF16)
    w_out16 = w_out.astype(BF16)
    w_ffn_in16 = w_ffn_in.astype(BF16)
    w_ffn_down16 = w_ffn_down.astype(BF16)
    for l in range(depth):
        proj, small = _in_proj(x2, attn_norm[l][None, :], w_in16, w_gates16, w_small16, l)
        y_a = _moba(proj, moba_q_norm[l][None, :], moba_k_norm[l][None, :], batch, seq)
        y_b = _deltanet(proj, small, _pad_rows(dn_conv[l], 8), _lane_row(dn_a_log[l], DN_HEADS),
                        _lane_row(dn_dt_bias[l], DN_HEADS), dn_out_norm[l][None, :], batch, seq)
        merged = _merge(y_a, y_b, w_a16, w_b16, proj, l)
        x2 = _res_mm(merged, w_out16, l, x2, MM_TN)
        x2 = _ffn(x2, ffn_norm[l], w_ffn_in16, ffn_conv[l], ffn_conv_bias[l], w_ffn_down16, l, seq)
    return x2.reshape(batch, seq, d)
```

```python
import functools

import jax
import jax.numpy as jnp
from jax import lax
from jax.experimental import pallas as pl
from jax.experimental.pallas import tpu as pltpu

F32 = jnp.float32
BF16 = jnp.bfloat16

D_MODEL = 2048
HEAD_DIM = 128
MOBA_HEADS = 8
MOBA_WIDTH = MOBA_HEADS * HEAD_DIM
MOBA_BLOCK = 256
MOBA_TOPK = 3
DN_HEADS = 8
DN_WIDTH = DN_HEADS * HEAD_DIM
DN_CONV = 4
D_FF = 5632
FFN_CONV = 3
NORM_EPS = 1e-6

MAIN_COLS = 3 * MOBA_WIDTH + 4 * DN_WIDTH
PROJ_COLS = MAIN_COLS + 2 * D_MODEL
SMALL_COLS = 128

VMEM_LIMIT_BYTES = 56 * 1024 * 1024

MM_TM = 1024
MM_TN = 1024
NORM_ROWS = 128
MM_ROWS = 256
ATTN_OUT_TM = 512
MOBA_GROUP = 8
MOBA_KV = 2
MOBA_VROWS = HEAD_DIM + 16
LOG2E = 1.4426950408889634
DN_TILE = 128
DN_SEQS = 2
DN_HIST = 16
FFN_TN = 512
FFN_ROWS = 256
NEG = -1e30


def _sigmoid(x):
    return 1.0 / (1.0 + jnp.exp(-x))


def _dot(a, b):
    return jnp.dot(a, b, preferred_element_type=F32)


def _dot_nt(a, b, precision=None):
    return lax.dot_general(a, b, (((1,), (1,)), ((), ())), precision=precision,
                           preferred_element_type=F32)


def _dot_tn(a, b):
    return lax.dot_general(a, b, (((0,), (0,)), ((), ())), preferred_element_type=F32)


def _params(*sem):
    return pltpu.CompilerParams(dimension_semantics=sem, vmem_limit_bytes=VMEM_LIMIT_BYTES)


def _rms_block(x_ref, g_ref, h_ref, row0, rows):
    for r in range(row0, row0 + rows, NORM_ROWS):
        x = x_ref[r:r + NORM_ROWS, :]
        ms = jnp.mean(x * x, axis=-1, keepdims=True)
        h_ref[r:r + NORM_ROWS, :] = (x * lax.rsqrt(ms + NORM_EPS) * g_ref[...]).astype(h_ref.dtype)


def _in_proj_kernel(n_main, x_ref, g_ref, wm_ref, wg_ref, ws_ref, o_ref, s_ref, h_ref):
    j = pl.program_id(1)
    tm = x_ref.shape[0]
    rc = min(MM_ROWS, tm)

    @pl.when(j == 0)
    def _():
        for r in range(0, tm, rc):
            _rms_block(x_ref, g_ref, h_ref, r, rc)
            h = h_ref[r:r + rc, :]
            s_ref[r:r + rc, :] = _dot(h, ws_ref[...])
            o_ref[r:r + rc, :] = _dot(h, wm_ref[...]).astype(o_ref.dtype)

    @pl.when(jnp.logical_and(j > 0, j < n_main))
    def _():
        o_ref[...] = _dot(h_ref[...], wm_ref[...]).astype(o_ref.dtype)

    @pl.when(j >= n_main)
    def _():
        o_ref[...] = _dot(h_ref[...], wg_ref[...]).astype(o_ref.dtype)


def _in_proj(x2, gain, w_in16, w_gates16, w_small16, layer):
    n, d = x2.shape
    tm = min(MM_TM, n)
    n_main = MAIN_COLS // MM_TN
    n_gate = w_gates16.shape[2] // MM_TN
    return pl.pallas_call(
        functools.partial(_in_proj_kernel, n_main),
        grid=(n // tm, n_main + n_gate),
        in_specs=[
            pl.BlockSpec((tm, d), lambda i, j: (i, 0)),
            pl.BlockSpec((1, d), lambda i, j: (0, 0)),
            pl.BlockSpec((None, d, MM_TN), lambda i, j: (layer, 0, jnp.minimum(j, n_main - 1))),
            pl.BlockSpec((None, d, MM_TN), lambda i, j: (layer, 0, jnp.maximum(j - n_main, 0))),
            pl.BlockSpec((None, d, SMALL_COLS), lambda i, j: (layer, 0, 0)),
        ],
        out_specs=[pl.BlockSpec((tm, MM_TN), lambda i, j: (i, j)),
                   pl.BlockSpec((tm, SMALL_COLS), lambda i, j: (i, 0))],
        out_shape=[jax.ShapeDtypeStruct((n, PROJ_COLS), BF16),
                   jax.ShapeDtypeStruct((n, SMALL_COLS), F32)],
        scratch_shapes=[pltpu.VMEM((tm, d), BF16)],
        compiler_params=_params("arbitrary", "arbitrary"), name="in_proj",
    )(x2, gain, w_in16, w_gates16, w_small16)


def _moba_kernel(q_ref, k_ref, v_ref, qg_ref, kg_ref, o_ref, kn_ref, vt_ref, km_ref, sel_ref):
    qi = pl.program_id(2)
    hp, nb = kn_ref.shape[0], kn_ref.shape[1]
    blk = MOBA_BLOCK
    heads = range(hp)

    def cols(h):
        return slice(h * HEAD_DIM, (h + 1) * HEAD_DIM)

    @pl.when(qi == 0)
    def _():
        tail_rows = lax.broadcasted_iota(jnp.int32, (MOBA_VROWS - HEAD_DIM, blk), 0)
        ones_tail = jnp.where(tail_rows == 0, 1.0, 0.0).astype(BF16)

        def prep(n, carry):
            r = pl.multiple_of(n * blk, blk)
            for h in heads:
                kb = k_ref[pl.ds(r, blk), cols(h)].astype(F32)
                ms = jnp.mean(kb * kb, axis=-1, keepdims=True)
                kn = kb * lax.rsqrt(ms + NORM_EPS) * kg_ref[...]
                kn_ref[h, n] = kn.astype(BF16)
                km_ref[h, pl.ds(n, 1), :] = jnp.mean(kn, axis=0, keepdims=True)
                vt_ref[h, n, 0:HEAD_DIM, :] = v_ref[pl.ds(r, blk), cols(h)].astype(F32).T.astype(BF16)
                vt_ref[h, n, HEAD_DIM:MOBA_VROWS, :] = ones_tail
            return carry

        lax.fori_loop(0, nb, prep, 0)

    q = [q_ref[:, cols(h)].astype(F32) for h in heads]
    qn = [t * lax.rsqrt(jnp.mean(t * t, axis=-1, keepdims=True) + NORM_EPS) * qg_ref[...] for t in q]

    gate = [_dot_nt(km_ref[h], qn[h], precision=lax.Precision.HIGHEST) for h in heads]
    rows = lax.broadcasted_iota(jnp.int32, (nb, blk), 0).astype(F32)
    past = rows < qi.astype(F32)
    gate = [jnp.where(past, g, NEG) for g in gate]
    sel = [jnp.zeros((nb, blk), F32) for _ in heads]
    for _ in range(MOBA_TOPK):
        mx = [jnp.max(g, axis=0, keepdims=True) for g in gate]
        idx = [jnp.min(jnp.where(gate[h] == mx[h], rows, float(nb)), axis=0, keepdims=True) for h in heads]
        pick = [jnp.logical_and(rows == idx[h], mx[h] > 0.5 * NEG) for h in heads]
        sel = [jnp.where(pick[h], 1.0, sel[h]) for h in heads]
        gate = [jnp.where(pick[h], NEG, gate[h]) for h in heads]
    for h in heads:
        sel_ref[h] = sel[h]

    qb = [(t * (HEAD_DIM ** -0.5 * LOG2E)).astype(BF16) for t in qn]

    s = [_dot_nt(kn_ref[h, qi], qb[h]) for h in heads]
    kidx = lax.broadcasted_iota(jnp.int32, (blk, blk), 0)
    qidx = lax.broadcasted_iota(jnp.int32, (blk, blk), 1)
    causal = kidx <= qidx
    s = [jnp.where(causal, t, NEG) for t in s]
    m0 = [jnp.max(t, axis=0, keepdims=True) for t in s]
    p = [jnp.exp2(s[h] - m0[h]) for h in heads]
    acc0 = [_dot(vt_ref[h, qi], p[h].astype(BF16)) for h in heads]

    kvb = MOBA_KV
    units = [(h, u) for h in heads for u in range(kvb)]

    def body(t, carry):
        m, acc = carry
        js = [jnp.minimum(t * kvb + u, nb - 1) for u in range(kvb)]
        sj = {(h, u): _dot_nt(kn_ref[h, js[u]], qb[h]) for h, u in units}
        sj = {(h, u): jnp.where(sel_ref[h, pl.ds(js[u], 1), :] > 0.0, sj[h, u], NEG) for h, u in units}
        bmax = {(h, u): jnp.max(sj[h, u], axis=0, keepdims=True) for h, u in units}
        m_new = []
        for h in heads:
            mh = m[h]
            for u in range(kvb):
                mh = jnp.maximum(mh, bmax[h, u])
            m_new.append(mh)
        alpha = [jnp.exp2(m[h] - m_new[h]) for h in heads]
        pj = {(h, u): jnp.exp2(sj[h, u] - m_new[h]).astype(BF16) for h, u in units}
        pv = {(h, u): _dot(vt_ref[h, js[u]], pj[h, u]) for h, u in units}
        acc_new = []
        for h in heads:
            ah = alpha[h] * acc[h]
            for u in range(kvb):
                ah = ah + pv[h, u]
            acc_new.append(ah)
        return m_new, acc_new

    _, acc = lax.fori_loop(0, (qi + (kvb - 1)) // kvb, body, (m0, acc0))
    for h in heads:
        out = acc[h][0:HEAD_DIM, :] / acc[h][HEAD_DIM:HEAD_DIM + 1, :]
        o_ref[:, cols(h)] = out.T.astype(o_ref.dtype)


def _moba(proj, q_gain, k_gain, batch, seq):
    n = batch * seq
    nb = seq // MOBA_BLOCK
    hp = MOBA_GROUP
    gw = hp * HEAD_DIM
    groups = MOBA_HEADS // hp
    return pl.pallas_call(
        _moba_kernel, grid=(batch, groups, nb),
        in_specs=[
            pl.BlockSpec((MOBA_BLOCK, gw), lambda b, g, i: (b * nb + i, g)),
            pl.BlockSpec((seq, gw), lambda b, g, i: (b, groups + g), pipeline_mode=pl.Buffered(1)),
            pl.BlockSpec((seq, gw), lambda b, g, i: (b, 2 * groups + g), pipeline_mode=pl.Buffered(1)),
            pl.BlockSpec((1, HEAD_DIM), lambda b, g, i: (0, 0)),
            pl.BlockSpec((1, HEAD_DIM), lambda b, g, i: (0, 0)),
        ],
        out_specs=pl.BlockSpec((MOBA_BLOCK, gw), lambda b, g, i: (b * nb + i, g)),
        out_shape=jax.ShapeDtypeStruct((n, MOBA_WIDTH), BF16),
        scratch_shapes=[
            pltpu.VMEM((hp, nb, MOBA_BLOCK, HEAD_DIM), BF16),
            pltpu.VMEM((hp, nb, MOBA_VROWS, MOBA_BLOCK), BF16),
            pltpu.VMEM((hp, nb, HEAD_DIM), F32),
            pltpu.VMEM((hp, nb, MOBA_BLOCK), F32),
        ],
        compiler_params=_params("arbitrary", "arbitrary", "arbitrary"), name="moba",
    )(proj, proj, proj, q_gain, k_gain)


def _dot_split(a16, b):
    hi = b.astype(BF16)
    lo = (b - hi.astype(F32)).astype(BF16)
    return _dot(a16, hi) + _dot(a16, lo)


def _dn_kernel(qkv_ref, z_ref, sm_ref, cw_ref, al_ref, dtb_ref, og_ref, o_ref, xe_ref, st_ref):
    c = DN_TILE
    ns, _, width = qkv_ref.shape
    seqs = range(ns)

    @pl.when(pl.program_id(1) == 0)
    def _():
        xe_ref[:, 0:DN_HIST, :] = jnp.zeros((ns, DN_HIST, width), BF16)
        st_ref[...] = jnp.zeros(st_ref.shape, F32)

    ri = lax.broadcasted_iota(jnp.int32, (c, c), 0)
    ci = lax.broadcasted_iota(jnp.int32, (c, c), 1)
    tril = ri >= ci
    strict = ri > ci
    eye = jnp.where(ri == ci, 1.0, 0.0)
    lower_blocks = []
    s = 1
    while s < c:
        same_2s = ((ri ^ ci) & ~(2 * s - 1)) == 0
        lower_blocks.append(jnp.logical_and(same_2s, jnp.logical_and((ri & s) != 0, (ci & s) == 0)))
        s *= 2
    tril16 = jnp.where(tril, 1.0, 0.0).astype(BF16)
    sr = lax.broadcasted_iota(jnp.int32, (c, DN_HIST + c), 0)
    sc = lax.broadcasted_iota(jnp.int32, (c, DN_HIST + c), 1)
    shifts = [jnp.where(sc == sr + (DN_HIST - d), 1.0, 0.0).astype(BF16) for d in range(1, DN_CONV)]


    ys = []
    for b in seqs:
        xe_ref[b, DN_HIST:DN_HIST + c, :] = qkv_ref[b]
        xe = xe_ref[b]
        y = cw_ref[DN_CONV - 1:DN_CONV, :] * qkv_ref[b].astype(F32)
        for d in range(1, DN_CONV):
            y = y + cw_ref[DN_CONV - 1 - d:DN_CONV - d, :] * _dot(shifts[d - 1], xe)
        xe_ref[b, 0:DN_HIST, :] = xe_ref[b, c:c + DN_HIST, :]
        ys.append(y * _sigmoid(y))

    sm = [sm_ref[b] for b in seqs]
    beta_all = [_sigmoid(t) for t in sm]
    xg = [t + dtb_ref[...] for t in sm]
    softplus = [jnp.maximum(t, 0.0) + jnp.log(1.0 + jnp.exp(-jnp.abs(t))) for t in xg]
    g_all = [-jnp.exp(al_ref[...]) * t for t in softplus]
    gcum = [_dot_split(tril16, t) for t in g_all]
    gcum_t = [t.T for t in gcum]

    units = [(b, h) for b in seqs for h in range(DN_HEADS)]

    def head_cols(b, base, h):
        return ys[b][:, base + h * HEAD_DIM:base + (h + 1) * HEAD_DIM]

    q = {u: head_cols(u[0], 0, u[1]) for u in units}
    k = {u: head_cols(u[0], DN_WIDTH, u[1]) for u in units}
    v = {u: head_cols(u[0], 2 * DN_WIDTH, u[1]) for u in units}
    q = {u: t * lax.rsqrt(jnp.sum(t * t, axis=-1, keepdims=True) + NORM_EPS) * (HEAD_DIM ** -0.5)
         for u, t in q.items()}
    k = {u: t * lax.rsqrt(jnp.sum(t * t, axis=-1, keepdims=True) + NORM_EPS) for u, t in k.items()}
    beta = {(b, h): beta_all[b][:, h:h + 1] for b, h in units}
    g_col = {(b, h): gcum[b][:, DN_HEADS + h:DN_HEADS + h + 1] for b, h in units}
    g_row = {(b, h): gcum_t[b][DN_HEADS + h:DN_HEADS + h + 1, :] for b, h in units}
    g_last = {u: t[c - 1:c, :] for u, t in g_col.items()}
    decay = {u: jnp.where(tril, jnp.exp(jnp.minimum(g_col[u] - g_row[u], 0.0)), 0.0) for u in units}
    e_g = {u: jnp.exp(t) for u, t in g_col.items()}
    kb = {u: k[u] * beta[u] for u in units}
    k16 = {u: t.astype(BF16) for u, t in k.items()}
    nmat = {u: jnp.where(strict, _dot_nt(kb[u].astype(BF16), k16[u]) * decay[u], 0.0) for u in units}
    attn = {u: (_dot_nt(q[u].astype(BF16), k16[u]) * decay[u]).astype(BF16) for u in units}

    rhs = {u: jnp.concatenate([v[u] * beta[u], kb[u] * e_g[u]], axis=1).astype(BF16) for u in units}
    n16 = {u: t.astype(BF16) for u, t in nmat.items()}
    tinv = {u: eye - jnp.where(lower_blocks[0], t, 0.0) for u, t in nmat.items()}
    for msk in lower_blocks[1:]:
        t16 = {u: t.astype(BF16) for u, t in tinv.items()}
        cm = {u: jnp.where(msk, t, jnp.zeros_like(t)) for u, t in n16.items()}
        p = {u: _dot(cm[u], t16[u]).astype(BF16) for u in units}
        tinv = {u: tinv[u] - _dot(t16[u], p[u]) for u in units}
    sol = {u: _dot(tinv[u].astype(BF16), rhs[u]) for u in units}

    state = {(b, h): st_ref[b, h] for b, h in units}
    s16 = {u: t.astype(BF16) for u, t in state.items()}
    v16 = {u: (sol[u][:, :HEAD_DIM] - _dot(sol[u][:, HEAD_DIM:].astype(BF16), s16[u])).astype(BF16)
           for u in units}
    o = {u: _dot((q[u] * e_g[u]).astype(BF16), s16[u]) + _dot(attn[u], v16[u]) for u in units}
    kd = {u: (k[u] * jnp.exp(g_last[u] - g_col[u])).astype(BF16) for u in units}
    for b, h in units:
        st_ref[b, h] = state[b, h] * jnp.exp(g_last[b, h]) + _dot_tn(kd[b, h], v16[b, h])
    for b, h in units:
        ou = o[b, h]
        on = ou * lax.rsqrt(jnp.mean(ou * ou, axis=-1, keepdims=True) + NORM_EPS) * og_ref[...]
        zh = z_ref[b, :, h * HEAD_DIM:(h + 1) * HEAD_DIM].astype(F32)
        o_ref[b, :, h * HEAD_DIM:(h + 1) * HEAD_DIM] = (on * (zh * _sigmoid(zh))).astype(o_ref.dtype)


def _deltanet(proj, small, conv_w, a_log_row, dt_bias_row, out_gain, batch, seq):
    nt = seq // DN_TILE
    ns = DN_SEQS if batch % DN_SEQS == 0 else 1
    qkv_w = 3 * DN_WIDTH
    qkv_blk = (3 * MOBA_WIDTH) // qkv_w
    z_blk = (3 * MOBA_WIDTH + qkv_w) // DN_WIDTH
    proj3 = proj.reshape(batch, seq, proj.shape[1])
    small3 = small.reshape(batch, seq, SMALL_COLS)
    y_b = pl.pallas_call(
        _dn_kernel, grid=(batch // ns, nt),
        in_specs=[
            pl.BlockSpec((ns, DN_TILE, qkv_w), lambda g, t: (g, t, qkv_blk)),
            pl.BlockSpec((ns, DN_TILE, DN_WIDTH), lambda g, t: (g, t, z_blk)),
            pl.BlockSpec((ns, DN_TILE, SMALL_COLS), lambda g, t: (g, t, 0)),
            pl.BlockSpec((8, qkv_w), lambda g, t: (0, 0)),
            pl.BlockSpec((1, SMALL_COLS), lambda g, t: (0, 0)),
            pl.BlockSpec((1, SMALL_COLS), lambda g, t: (0, 0)),
            pl.BlockSpec((1, HEAD_DIM), lambda g, t: (0, 0)),
        ],
        out_specs=pl.BlockSpec((ns, DN_TILE, DN_WIDTH), lambda g, t: (g, t, 0)),
        out_shape=jax.ShapeDtypeStruct((batch, seq, DN_WIDTH), BF16),
        scratch_shapes=[
            pltpu.VMEM((ns, DN_HIST + DN_TILE, qkv_w), BF16),
            pltpu.VMEM((ns, DN_HEADS, HEAD_DIM, HEAD_DIM), F32),
        ],
        compiler_params=_params("arbitrary", "arbitrary"), name="deltanet",
    )(proj3, proj3, small3, conv_w, a_log_row, dt_bias_row, out_gain)
    return y_b.reshape(batch * seq, DN_WIDTH)


def _attn_out_kernel(ya_ref, yb_ref, wa_ref, wb_ref, ga0_ref, ga1_ref, gb0_ref, gb1_ref, wo_ref, x_ref,
                     o_ref, mg_ref):
    tm = ya_ref.shape[0]
    rc = min(MM_ROWS, tm)
    half = D_MODEL // 2
    gates = ((ga0_ref, gb0_ref), (ga1_ref, gb1_ref))

    def dots(r, c):
        rows = slice(r * rc, (r + 1) * rc)
        cols = slice(c * half, (c + 1) * half)
        return _dot(ya_ref[rows, :], wa_ref[:, cols]), _dot(yb_ref[rows, :], wb_ref[:, cols])

    def combine(r, c, ab):
        rows = slice(r * rc, (r + 1) * rc)
        ga = _sigmoid(gates[c][0][rows, :].astype(F32))
        gb = _sigmoid(gates[c][1][rows, :].astype(F32))
        mg_ref[rows, c * half:(c + 1) * half] = (ga * ab[0] + gb * ab[1]).astype(mg_ref.dtype)

    def project(r):
        rows = slice(r * rc, (r + 1) * rc)
        o_ref[rows, :] = x_ref[rows, :] + _dot(mg_ref[rows, :], wo_ref[...])

    units = [(r, c) for r in range(tm // rc) for c in range(2)]
    pending = None
    for r, c in units:
        ab = dots(r, c)
        if pending is not None:
            combine(*pending)
            if pending[1] == 1:
                project(pending[0])
        pending = (r, c, ab)
    combine(*pending)
    project(pending[0])


def _attn_out(y_a, y_b, w_a16, w_b16, proj, w_out16, layer, x2):
    n = y_a.shape[0]
    tm = min(ATTN_OUT_TM, n)
    half = D_MODEL // 2
    ga_blk = MAIN_COLS // half
    gb_blk = ga_blk + 2
    resident = dict(pipeline_mode=pl.Buffered(1))
    return pl.pallas_call(
        _attn_out_kernel, grid=(n // tm,),
        in_specs=[
            pl.BlockSpec((tm, MOBA_WIDTH), lambda i: (i, 0)),
            pl.BlockSpec((tm, DN_WIDTH), lambda i: (i, 0)),
            pl.BlockSpec((None, MOBA_WIDTH, D_MODEL), lambda i: (layer, 0, 0), **resident),
            pl.BlockSpec((None, DN_WIDTH, D_MODEL), lambda i: (layer, 0, 0), **resident),
            pl.BlockSpec((tm, half), lambda i: (i, ga_blk)),
            pl.BlockSpec((tm, half), lambda i: (i, ga_blk + 1)),
            pl.BlockSpec((tm, half), lambda i: (i, gb_blk)),
            pl.BlockSpec((tm, half), lambda i: (i, gb_blk + 1)),
            pl.BlockSpec((None, D_MODEL, D_MODEL), lambda i: (layer, 0, 0), **resident),
            pl.BlockSpec((tm, D_MODEL), lambda i: (i, 0)),
        ],
        out_specs=pl.BlockSpec((tm, D_MODEL), lambda i: (i, 0)),
        out_shape=jax.ShapeDtypeStruct((n, D_MODEL), F32),
        scratch_shapes=[pltpu.VMEM((tm, D_MODEL), BF16)],
        compiler_params=_params("arbitrary"), name="attn_out",
    )(y_a, y_b, w_a16, w_b16, proj, proj, proj, proj, w_out16, x2)


def _res_mm_kernel(a_ref, w_ref, x_ref, o_ref):
    o_ref[...] = x_ref[...] + _dot(a_ref[...], w_ref[...])


def _res_mm(a, w, layer, x2, tn):
    n, k = a.shape
    cols = w.shape[2]
    tm = min(MM_TM, n)
    return pl.pallas_call(
        _res_mm_kernel, grid=(n // tm, cols // tn),
        in_specs=[
            pl.BlockSpec((tm, k), lambda i, j: (i, 0)),
            pl.BlockSpec((None, k, tn), lambda i, j: (layer, 0, j)),
            pl.BlockSpec((tm, tn), lambda i, j: (i, j)),
        ],
        out_specs=pl.BlockSpec((tm, tn), lambda i, j: (i, j)),
        out_shape=jax.ShapeDtypeStruct((n, cols), F32),
        compiler_params=_params("arbitrary", "arbitrary"), name="res_mm",
    )(a, w, x2)


def _ffn_in_kernel(tiles_per_seq, x_ref, g_ref, wg_ref, wu_ref, cw_ref, cb_ref, o_ref,
                   h_ref, cs_ref, tail_ref):
    i = pl.program_id(0)
    j = pl.program_id(1)
    tm = x_ref.shape[0]

    seq_start = (i % tiles_per_seq) == 0

    @pl.when(seq_start)
    def _():
        cs_ref[0:8, :] = jnp.zeros((8, cs_ref.shape[1]), F32)

    @pl.when(jnp.logical_not(seq_start))
    def _():
        cs_ref[0:8, :] = tail_ref[j]

    rc = min(FFN_ROWS, tm)

    def epilogue(r, up):
        y = cb_ref[...] + cw_ref[0:1, :] * cs_ref[6 + r * rc:6 + (r + 1) * rc, :]
        for t in range(1, FFN_CONV):
            y = y + cw_ref[t:t + 1, :] * cs_ref[6 + t + r * rc:6 + t + (r + 1) * rc, :]
        o_ref[r * rc:(r + 1) * rc, :] = (y * _sigmoid(y) * up).astype(o_ref.dtype)

    def step(normalise):
        def dots(r):
            if normalise:
                _rms_block(x_ref, g_ref, h_ref, r * rc, rc)
            h = h_ref[r * rc:(r + 1) * rc, :]
            cs_ref[8 + r * rc:8 + (r + 1) * rc, :] = _dot(h, wg_ref[...])
            return _dot(h, wu_ref[...])

        up_prev = dots(0)
        for r in range(1, tm // rc):
            up_next = dots(r)
            epilogue(r - 1, up_prev)
            up_prev = up_next
        tail_ref[j] = cs_ref[tm:tm + 8, :]
        epilogue(tm // rc - 1, up_prev)

    @pl.when(j == 0)
    def _():
        step(True)

    @pl.when(j > 0)
    def _():
        step(False)


def _ffn_in(x2, gain, w_in16, layer, conv_w, conv_b, seq):
    n, d = x2.shape
    tm = min(MM_TM, seq)
    nj = D_FF // FFN_TN
    return pl.pallas_call(
        functools.partial(_ffn_in_kernel, seq // tm),
        grid=(n // tm, nj),
        in_specs=[
            pl.BlockSpec((tm, d), lambda i, j: (i, 0)),
            pl.BlockSpec((1, d), lambda i, j: (0, 0)),
            pl.BlockSpec((None, d, FFN_TN), lambda i, j: (layer, 0, j)),
            pl.BlockSpec((None, d, FFN_TN), lambda i, j: (layer, 0, nj + j)),
            pl.BlockSpec((8, FFN_TN), lambda i, j: (0, j)),
            pl.BlockSpec((1, FFN_TN), lambda i, j: (0, j)),
        ],
        out_specs=pl.BlockSpec((tm, FFN_TN), lambda i, j: (i, j)),
        out_shape=jax.ShapeDtypeStruct((n, D_FF), BF16),
        scratch_shapes=[
            pltpu.VMEM((tm, d), BF16),
            pltpu.VMEM((tm + 8, FFN_TN), F32),
            pltpu.VMEM((nj, 8, FFN_TN), F32),
        ],
        compiler_params=_params("arbitrary", "arbitrary"), name="ffn_in",
    )(x2, gain, w_in16, w_in16, conv_w, conv_b)


def _ffn(x2, gain, w_ffn_in16, conv_w, conv_b, w_down16, layer, seq):
    act = _ffn_in(x2, gain[None, :], w_ffn_in16, layer, _pad_rows(conv_w, 8), conv_b[None, :], seq)
    return _res_mm(act, w_down16, layer, x2, FFN_TN)


def _pad_rows(w, rows):
    return jnp.pad(w, ((0, rows - w.shape[0]), (0, 0)))


def _lane_row(v, offset):
    return jnp.zeros((1, SMALL_COLS), F32).at[0, offset:offset + v.shape[0]].set(v.astype(F32))


def kernel(x, attn_norm, w_in, moba_q_norm, moba_k_norm, dn_conv, dn_a_log, dn_dt_bias, dn_out_norm,
           w_branch_a, w_branch_b, w_out, ffn_norm, w_ffn_in, ffn_conv, ffn_conv_bias, w_ffn_down):
    batch, seq, d = x.shape
    depth = w_in.shape[0]
    n = batch * seq
    assert d == D_MODEL and seq % MOBA_BLOCK == 0 and seq % DN_TILE == 0 and n % min(MM_TM, n) == 0
    small_w = 2 * DN_HEADS
    x2 = x.reshape(n, d)
    w_in16 = w_in.astype(BF16)
    w_gates16 = w_in16[:, :, MAIN_COLS + small_w:]
    w_small16 = jnp.pad(w_in16[:, :, MAIN_COLS:MAIN_COLS + small_w], ((0, 0), (0, 0), (0, SMALL_COLS - small_w)))
    w_a16 = w_branch_a.astype(BF16)
    w_b16 = w_branch_b.astype(BF16)
    w_out16 = w_out.astype(BF16)
    w_ffn_in16 = w_ffn_in.astype(BF16)
    w_ffn_down16 = w_ffn_down.astype(BF16)
    for l in range(depth):
        proj, small = _in_proj(x2, attn_norm[l][None, :], w_in16, w_gates16, w_small16, l)
        y_a = _moba(proj, moba_q_norm[l][None, :], moba_k_norm[l][None, :], batch, seq)
        y_b = _deltanet(proj, small, _pad_rows(dn_conv[l], 8), _lane_row(dn_a_log[l], DN_HEADS),
                        _lane_row(dn_dt_bias[l], DN_HEADS), dn_out_norm[l][None, :], batch, seq)
        x2 = _attn_out(y_a, y_b, w_a16, w_b16, proj, w_out16, l, x2)
        x2 = _ffn(x2, ffn_norm[l], w_ffn_in16, ffn_conv[l], ffn_conv_bias[l], w_ffn_down16, l, seq)
    return x2.reshape(batch, seq, d)
```

```python
import functools

import jax
import jax.numpy as jnp
from jax import lax
from jax.experimental import pallas as pl
from jax.experimental.pallas import tpu as pltpu

F32 = jnp.float32
BF16 = jnp.bfloat16

D_MODEL = 2048
HEAD_DIM = 128
MOBA_HEADS = 8
MOBA_WIDTH = MOBA_HEADS * HEAD_DIM
MOBA_BLOCK = 256
MOBA_TOPK = 3
DN_HEADS = 8
DN_WIDTH = DN_HEADS * HEAD_DIM
DN_CONV = 4
D_FF = 5632
FFN_CONV = 3
NORM_EPS = 1e-6

MAIN_COLS = 3 * MOBA_WIDTH + 4 * DN_WIDTH
PROJ_COLS = MAIN_COLS + 2 * D_MODEL
SMALL_COLS = 128

VMEM_LIMIT_BYTES = 56 * 1024 * 1024

MM_TM = 1024
MM_TN = 1024
NORM_ROWS = 128
MM_ROWS = 256
ATTN_OUT_TM = 512
MOBA_GROUP = 8
MOBA_KV = 2
MOBA_VROWS = HEAD_DIM + 16
LOG2E = 1.4426950408889634
DN_TILE = 128
DN_SEQS = 2
DN_HIST = 16
FFN_TN = 512
FFN_ROWS = 256
NEG = -1e30
NEG16 = -1e30


def _sigmoid(x):
    return 1.0 / (1.0 + jnp.exp(-x))


def _dot(a, b):
    return jnp.dot(a, b, preferred_element_type=F32)


def _dot_nt(a, b, precision=None):
    return lax.dot_general(a, b, (((1,), (1,)), ((), ())), precision=precision,
                           preferred_element_type=F32)


def _dot_tn(a, b):
    return lax.dot_general(a, b, (((0,), (0,)), ((), ())), preferred_element_type=F32)


def _params(*sem):
    return pltpu.CompilerParams(dimension_semantics=sem, vmem_limit_bytes=VMEM_LIMIT_BYTES)


def _rms_block(x_ref, g_ref, h_ref, row0, rows):
    for r in range(row0, row0 + rows, NORM_ROWS):
        x = x_ref[r:r + NORM_ROWS, :]
        ms = jnp.mean(x * x, axis=-1, keepdims=True)
        h_ref[r:r + NORM_ROWS, :] = (x * lax.rsqrt(ms + NORM_EPS) * g_ref[...]).astype(h_ref.dtype)


def _in_proj_kernel(n_main, x_ref, g_ref, wm_ref, wg_ref, ws_ref, o_ref, s_ref, h_ref):
    j = pl.program_id(1)
    tm = x_ref.shape[0]
    rc = min(MM_ROWS, tm)

    @pl.when(j == 0)
    def _():
        for r in range(0, tm, rc):
            _rms_block(x_ref, g_ref, h_ref, r, rc)
            h = h_ref[r:r + rc, :]
            s_ref[r:r + rc, :] = _dot(h, ws_ref[...])
            o_ref[r:r + rc, :] = _dot(h, wm_ref[...]).astype(o_ref.dtype)

    @pl.when(jnp.logical_and(j > 0, j < n_main))
    def _():
        o_ref[...] = _dot(h_ref[...], wm_ref[...]).astype(o_ref.dtype)

    @pl.when(j >= n_main)
    def _():
        o_ref[...] = _dot(h_ref[...], wg_ref[...]).astype(o_ref.dtype)


def _in_proj(x2, gain, w_in16, w_gates16, w_small16, layer):
    n, d = x2.shape
    tm = min(MM_TM, n)
    n_main = MAIN_COLS // MM_TN
    n_gate = w_gates16.shape[2] // MM_TN
    return pl.pallas_call(
        functools.partial(_in_proj_kernel, n_main),
        grid=(n // tm, n_main + n_gate),
        in_specs=[
            pl.BlockSpec((tm, d), lambda i, j: (i, 0)),
            pl.BlockSpec((1, d), lambda i, j: (0, 0)),
            pl.BlockSpec((None, d, MM_TN), lambda i, j: (layer, 0, jnp.minimum(j, n_main - 1))),
            pl.BlockSpec((None, d, MM_TN), lambda i, j: (layer, 0, jnp.maximum(j - n_main, 0))),
            pl.BlockSpec((None, d, SMALL_COLS), lambda i, j: (layer, 0, 0)),
        ],
        out_specs=[pl.BlockSpec((tm, MM_TN), lambda i, j: (i, j)),
                   pl.BlockSpec((tm, SMALL_COLS), lambda i, j: (i, 0))],
        out_shape=[jax.ShapeDtypeStruct((n, PROJ_COLS), BF16),
                   jax.ShapeDtypeStruct((n, SMALL_COLS), F32)],
        scratch_shapes=[pltpu.VMEM((tm, d), BF16)],
        compiler_params=_params("arbitrary", "arbitrary"), name="in_proj",
    )(x2, gain, w_in16, w_gates16, w_small16)


def _moba_kernel(q_ref, k_ref, v_ref, qg_ref, kg_ref, o_ref, kn_ref, vt_ref, km_ref, sel_ref):
    qi = pl.program_id(2)
    hp, nb = kn_ref.shape[0], kn_ref.shape[1]
    blk = MOBA_BLOCK
    heads = range(hp)

    def cols(h):
        return slice(h * HEAD_DIM, (h + 1) * HEAD_DIM)

    @pl.when(qi == 0)
    def _():
        tail_rows = lax.broadcasted_iota(jnp.int32, (MOBA_VROWS - HEAD_DIM, blk), 0)
        ones_tail = jnp.where(tail_rows == 0, 1.0, 0.0).astype(BF16)

        def prep(n, carry):
            r = pl.multiple_of(n * blk, blk)
            for h in heads:
                kb = k_ref[pl.ds(r, blk), cols(h)].astype(F32)
                ms = jnp.mean(kb * kb, axis=-1, keepdims=True)
                kn = kb * lax.rsqrt(ms + NORM_EPS) * kg_ref[...]
                kn_ref[h, n] = kn.astype(BF16)
                km_ref[h, pl.ds(n, 1), :] = jnp.mean(kn, axis=0, keepdims=True)
                vt_ref[h, n, 0:HEAD_DIM, :] = v_ref[pl.ds(r, blk), cols(h)].astype(F32).T.astype(BF16)
                vt_ref[h, n, HEAD_DIM:MOBA_VROWS, :] = ones_tail
            return carry

        lax.fori_loop(0, nb, prep, 0)

    q = [q_ref[:, cols(h)].astype(F32) for h in heads]
    qn = [t * lax.rsqrt(jnp.mean(t * t, axis=-1, keepdims=True) + NORM_EPS) * qg_ref[...] for t in q]

    gate = [_dot_nt(km_ref[h], qn[h], precision=lax.Precision.HIGHEST) for h in heads]
    rows = lax.broadcasted_iota(jnp.int32, (nb, blk), 0).astype(F32)
    past = rows < qi.astype(F32)
    gate = [jnp.where(past, g, NEG) for g in gate]
    sel = [jnp.zeros((nb, blk), F32) for _ in heads]
    for _ in range(MOBA_TOPK):
        mx = [jnp.max(g, axis=0, keepdims=True) for g in gate]
        idx = [jnp.min(jnp.where(gate[h] == mx[h], rows, float(nb)), axis=0, keepdims=True) for h in heads]
        pick = [jnp.logical_and(rows == idx[h], mx[h] > 0.5 * NEG) for h in heads]
        sel = [jnp.where(pick[h], 1.0, sel[h]) for h in heads]
        gate = [jnp.where(pick[h], NEG, gate[h]) for h in heads]
    for h in heads:
        sel_ref[h] = sel[h]

    qb = [(t * (HEAD_DIM ** -0.5 * LOG2E)).astype(BF16) for t in qn]

    s = [_dot_nt(kn_ref[h, qi], qb[h]) for h in heads]
    kidx = lax.broadcasted_iota(jnp.int32, (blk, blk), 0)
    qidx = lax.broadcasted_iota(jnp.int32, (blk, blk), 1)
    causal = kidx <= qidx
    s = [jnp.where(causal, t, NEG) for t in s]
    m0 = [jnp.max(t, axis=0, keepdims=True) for t in s]
    p = [jnp.exp2(s[h] - m0[h]) for h in heads]
    acc0 = [_dot(vt_ref[h, qi], p[h].astype(BF16)) for h in heads]

    kvb = MOBA_KV
    units = [(h, u) for h in heads for u in range(kvb)]

    def body(t, carry):
        m, acc = carry
        js = [jnp.minimum(t * kvb + u, nb - 1) for u in range(kvb)]
        dj = {(h, u): (_dot_nt(kn_ref[h, js[u]], qb[h]) - m[h]).astype(BF16) for h, u in units}
        keep = {(h, u): (sel_ref[h, pl.ds(js[u], 1), :] > 0.0) for h, u in units}
        dj = {(h, u): jnp.where(keep[h, u], dj[h, u], jnp.full_like(dj[h, u], NEG16)) for h, u in units}
        bmax = {(h, u): jnp.max(dj[h, u], axis=0, keepdims=True) for h, u in units}
        up = []
        for h in heads:
            uh = jnp.zeros_like(bmax[h, 0])
            for u in range(kvb):
                uh = jnp.maximum(uh, bmax[h, u])
            up.append(uh)
        up32 = [t.astype(F32) for t in up]
        m_new = [m[h] + up32[h] for h in heads]
        alpha = [jnp.exp2(-up32[h]) for h in heads]
        pj = {(h, u): jnp.exp2(dj[h, u] - up[h]) for h, u in units}
        pv = {(h, u): _dot(vt_ref[h, js[u]], pj[h, u]) for h, u in units}
        acc_new = []
        for h in heads:
            ah = alpha[h] * acc[h]
            for u in range(kvb):
                ah = ah + pv[h, u]
            acc_new.append(ah)
        return m_new, acc_new

    _, acc = lax.fori_loop(0, (qi + (kvb - 1)) // kvb, body, (m0, acc0))
    for h in heads:
        out = acc[h][0:HEAD_DIM, :] / acc[h][HEAD_DIM:HEAD_DIM + 1, :]
        o_ref[:, cols(h)] = out.T.astype(o_ref.dtype)


def _moba(proj, q_gain, k_gain, batch, seq):
    n = batch * seq
    nb = seq // MOBA_BLOCK
    hp = MOBA_GROUP
    gw = hp * HEAD_DIM
    groups = MOBA_HEADS // hp
    return pl.pallas_call(
        _moba_kernel, grid=(batch, groups, nb),
        in_specs=[
            pl.BlockSpec((MOBA_BLOCK, gw), lambda b, g, i: (b * nb + i, g)),
            pl.BlockSpec((seq, gw), lambda b, g, i: (b, groups + g), pipeline_mode=pl.Buffered(1)),
            pl.BlockSpec((seq, gw), lambda b, g, i: (b, 2 * groups + g), pipeline_mode=pl.Buffered(1)),
            pl.BlockSpec((1, HEAD_DIM), lambda b, g, i: (0, 0)),
            pl.BlockSpec((1, HEAD_DIM), lambda b, g, i: (0, 0)),
        ],
        out_specs=pl.BlockSpec((MOBA_BLOCK, gw), lambda b, g, i: (b * nb + i, g)),
        out_shape=jax.ShapeDtypeStruct((n, MOBA_WIDTH), BF16),
        scratch_shapes=[
            pltpu.VMEM((hp, nb, MOBA_BLOCK, HEAD_DIM), BF16),
            pltpu.VMEM((hp, nb, MOBA_VROWS, MOBA_BLOCK), BF16),
            pltpu.VMEM((hp, nb, HEAD_DIM), F32),
            pltpu.VMEM((hp, nb, MOBA_BLOCK), F32),
        ],
        compiler_params=_params("arbitrary", "arbitrary", "arbitrary"), name="moba",
    )(proj, proj, proj, q_gain, k_gain)


def _dot_split(a16, b):
    hi = b.astype(BF16)
    lo = (b - hi.astype(F32)).astype(BF16)
    return _dot(a16, hi) + _dot(a16, lo)


def _dn_kernel(qkv_ref, z_ref, sm_ref, cw_ref, al_ref, dtb_ref, og_ref, o_ref, xe_ref, st_ref):
    c = DN_TILE
    ns, _, width = qkv_ref.shape
    seqs = range(ns)

    @pl.when(pl.program_id(1) == 0)
    def _():
        xe_ref[:, 0:DN_HIST, :] = jnp.zeros((ns, DN_HIST, width), BF16)
        st_ref[...] = jnp.zeros(st_ref.shape, F32)

    ri = lax.broadcasted_iota(jnp.int32, (c, c), 0)
    ci = lax.broadcasted_iota(jnp.int32, (c, c), 1)
    tril = ri >= ci
    strict = ri > ci
    eye = jnp.where(ri == ci, 1.0, 0.0)
    lower_blocks = []
    s = 1
    while s < c:
        same_2s = ((ri ^ ci) & ~(2 * s - 1)) == 0
        lower_blocks.append(jnp.logical_and(same_2s, jnp.logical_and((ri & s) != 0, (ci & s) == 0)))
        s *= 2
    tril16 = jnp.where(tril, 1.0, 0.0).astype(BF16)
    sr = lax.broadcasted_iota(jnp.int32, (c, DN_HIST + c), 0)
    sc = lax.broadcasted_iota(jnp.int32, (c, DN_HIST + c), 1)
    shifts = [jnp.where(sc == sr + (DN_HIST - d), 1.0, 0.0).astype(BF16) for d in range(1, DN_CONV)]


    ys = []
    for b in seqs:
        xe_ref[b, DN_HIST:DN_HIST + c, :] = qkv_ref[b]
        xe = xe_ref[b]
        y = cw_ref[DN_CONV - 1:DN_CONV, :] * qkv_ref[b].astype(F32)
        for d in range(1, DN_CONV):
            y = y + cw_ref[DN_CONV - 1 - d:DN_CONV - d, :] * _dot(shifts[d - 1], xe)
        xe_ref[b, 0:DN_HIST, :] = xe_ref[b, c:c + DN_HIST, :]
        ys.append(y * _sigmoid(y))

    sm = [sm_ref[b] for b in seqs]
    beta_all = [_sigmoid(t) for t in sm]
    xg = [t + dtb_ref[...] for t in sm]
    softplus = [jnp.maximum(t, 0.0) + jnp.log(1.0 + jnp.exp(-jnp.abs(t))) for t in xg]
    g_all = [-jnp.exp(al_ref[...]) * t for t in softplus]
    gcum = [_dot_split(tril16, t) for t in g_all]
    gcum_t = [t.T for t in gcum]

    units = [(b, h) for b in seqs for h in range(DN_HEADS)]

    def head_cols(b, base, h):
        return ys[b][:, base + h * HEAD_DIM:base + (h + 1) * HEAD_DIM]

    q = {u: head_cols(u[0], 0, u[1]) for u in units}
    k = {u: head_cols(u[0], DN_WIDTH, u[1]) for u in units}
    v = {u: head_cols(u[0], 2 * DN_WIDTH, u[1]) for u in units}
    q = {u: t * lax.rsqrt(jnp.sum(t * t, axis=-1, keepdims=True) + NORM_EPS) * (HEAD_DIM ** -0.5)
         for u, t in q.items()}
    k = {u: t * lax.rsqrt(jnp.sum(t * t, axis=-1, keepdims=True) + NORM_EPS) for u, t in k.items()}
    beta = {(b, h): beta_all[b][:, h:h + 1] for b, h in units}
    g_col = {(b, h): gcum[b][:, DN_HEADS + h:DN_HEADS + h + 1] for b, h in units}
    g_row = {(b, h): gcum_t[b][DN_HEADS + h:DN_HEADS + h + 1, :] for b, h in units}
    g_last = {u: t[c - 1:c, :] for u, t in g_col.items()}
    decay = {u: jnp.where(tril, jnp.exp(jnp.minimum(g_col[u] - g_row[u], 0.0)), 0.0) for u in units}
    e_g = {u: jnp.exp(t) for u, t in g_col.items()}
    kb = {u: k[u] * beta[u] for u in units}
    k16 = {u: t.astype(BF16) for u, t in k.items()}
    nmat = {u: jnp.where(strict, _dot_nt(kb[u].astype(BF16), k16[u]) * decay[u], 0.0) for u in units}
    attn = {u: (_dot_nt(q[u].astype(BF16), k16[u]) * decay[u]).astype(BF16) for u in units}

    rhs = {u: jnp.concatenate([v[u] * beta[u], kb[u] * e_g[u]], axis=1).astype(BF16) for u in units}
    n16 = {u: t.astype(BF16) for u, t in nmat.items()}
    tinv = {u: eye - jnp.where(lower_blocks[0], t, 0.0) for u, t in nmat.items()}
    for msk in lower_blocks[1:]:
        t16 = {u: t.astype(BF16) for u, t in tinv.items()}
        cm = {u: jnp.where(msk, t, jnp.zeros_like(t)) for u, t in n16.items()}
        p = {u: _dot(cm[u], t16[u]).astype(BF16) for u in units}
        tinv = {u: tinv[u] - _dot(t16[u], p[u]) for u in units}
    sol = {u: _dot(tinv[u].astype(BF16), rhs[u]) for u in units}

    state = {(b, h): st_ref[b, h] for b, h in units}
    s16 = {u: t.astype(BF16) for u, t in state.items()}
    v16 = {u: (sol[u][:, :HEAD_DIM] - _dot(sol[u][:, HEAD_DIM:].astype(BF16), s16[u])).astype(BF16)
           for u in units}
    o = {u: _dot((q[u] * e_g[u]).astype(BF16), s16[u]) + _dot(attn[u], v16[u]) for u in units}
    kd = {u: (k[u] * jnp.exp(g_last[u] - g_col[u])).astype(BF16) for u in units}
    for b, h in units:
        st_ref[b, h] = state[b, h] * jnp.exp(g_last[b, h]) + _dot_tn(kd[b, h], v16[b, h])
    for b, h in units:
        ou = o[b, h]
        on = ou * lax.rsqrt(jnp.mean(ou * ou, axis=-1, keepdims=True) + NORM_EPS) * og_ref[...]
        zh = z_ref[b, :, h * HEAD_DIM:(h + 1) * HEAD_DIM].astype(F32)
        o_ref[b, :, h * HEAD_DIM:(h + 1) * HEAD_DIM] = (on * (zh * _sigmoid(zh))).astype(o_ref.dtype)


def _deltanet(proj, small, conv_w, a_log_row, dt_bias_row, out_gain, batch, seq):
    nt = seq // DN_TILE
    ns = DN_SEQS if batch % DN_SEQS == 0 else 1
    qkv_w = 3 * DN_WIDTH
    qkv_blk = (3 * MOBA_WIDTH) // qkv_w
    z_blk = (3 * MOBA_WIDTH + qkv_w) // DN_WIDTH
    proj3 = proj.reshape(batch, seq, proj.shape[1])
    small3 = small.reshape(batch, seq, SMALL_COLS)
    y_b = pl.pallas_call(
        _dn_kernel, grid=(batch // ns, nt),
        in_specs=[
            pl.BlockSpec((ns, DN_TILE, qkv_w), lambda g, t: (g, t, qkv_blk)),
            pl.BlockSpec((ns, DN_TILE, DN_WIDTH), lambda g, t: (g, t, z_blk)),
            pl.BlockSpec((ns, DN_TILE, SMALL_COLS), lambda g, t: (g, t, 0)),
            pl.BlockSpec((8, qkv_w), lambda g, t: (0, 0)),
            pl.BlockSpec((1, SMALL_COLS), lambda g, t: (0, 0)),
            pl.BlockSpec((1, SMALL_COLS), lambda g, t: (0, 0)),
            pl.BlockSpec((1, HEAD_DIM), lambda g, t: (0, 0)),
        ],
        out_specs=pl.BlockSpec((ns, DN_TILE, DN_WIDTH), lambda g, t: (g, t, 0)),
        out_shape=jax.ShapeDtypeStruct((batch, seq, DN_WIDTH), BF16),
        scratch_shapes=[
            pltpu.VMEM((ns, DN_HIST + DN_TILE, qkv_w), BF16),
            pltpu.VMEM((ns, DN_HEADS, HEAD_DIM, HEAD_DIM), F32),
        ],
        compiler_params=_params("arbitrary", "arbitrary"), name="deltanet",
    )(proj3, proj3, small3, conv_w, a_log_row, dt_bias_row, out_gain)
    return y_b.reshape(batch * seq, DN_WIDTH)


def _attn_out_kernel(ya_ref, yb_ref, wa_ref, wb_ref, ga0_ref, ga1_ref, gb0_ref, gb1_ref, wo_ref, x_ref,
                     o_ref, mg_ref):
    tm = ya_ref.shape[0]
    rc = min(MM_ROWS, tm)
    half = D_MODEL // 2
    gates = ((ga0_ref, gb0_ref), (ga1_ref, gb1_ref))

    def dots(r, c):
        rows = slice(r * rc, (r + 1) * rc)
        cols = slice(c * half, (c + 1) * half)
        return _dot(ya_ref[rows, :], wa_ref[:, cols]), _dot(yb_ref[rows, :], wb_ref[:, cols])

    def combine(r, c, ab):
        rows = slice(r * rc, (r + 1) * rc)
        ga = _sigmoid(gates[c][0][rows, :].astype(F32))
        gb = _sigmoid(gates[c][1][rows, :].astype(F32))
        mg_ref[rows, c * half:(c + 1) * half] = (ga * ab[0] + gb * ab[1]).astype(mg_ref.dtype)

    def project(r):
        rows = slice(r * rc, (r + 1) * rc)
        o_ref[rows, :] = x_ref[rows, :] + _dot(mg_ref[rows, :], wo_ref[...])

    units = [(r, c) for r in range(tm // rc) for c in range(2)]
    pending = None
    for r, c in units:
        ab = dots(r, c)
        if pending is not None:
            combine(*pending)
            if pending[1] == 1:
                project(pending[0])
        pending = (r, c, ab)
    combine(*pending)
    project(pending[0])


def _attn_out(y_a, y_b, w_a16, w_b16, proj, w_out16, layer, x2):
    n = y_a.shape[0]
    tm = min(ATTN_OUT_TM, n)
    half = D_MODEL // 2
    ga_blk = MAIN_COLS // half
    gb_blk = ga_blk + 2
    resident = dict(pipeline_mode=pl.Buffered(1))
    return pl.pallas_call(
        _attn_out_kernel, grid=(n // tm,),
        in_specs=[
            pl.BlockSpec((tm, MOBA_WIDTH), lambda i: (i, 0)),
            pl.BlockSpec((tm, DN_WIDTH), lambda i: (i, 0)),
            pl.BlockSpec((None, MOBA_WIDTH, D_MODEL), lambda i: (layer, 0, 0), **resident),
            pl.BlockSpec((None, DN_WIDTH, D_MODEL), lambda i: (layer, 0, 0), **resident),
            pl.BlockSpec((tm, half), lambda i: (i, ga_blk)),
            pl.BlockSpec((tm, half), lambda i: (i, ga_blk + 1)),
            pl.BlockSpec((tm, half), lambda i: (i, gb_blk)),
            pl.BlockSpec((tm, half), lambda i: (i, gb_blk + 1)),
            pl.BlockSpec((None, D_MODEL, D_MODEL), lambda i: (layer, 0, 0), **resident),
            pl.BlockSpec((tm, D_MODEL), lambda i: (i, 0)),
        ],
        out_specs=pl.BlockSpec((tm, D_MODEL), lambda i: (i, 0)),
        out_shape=jax.ShapeDtypeStruct((n, D_MODEL), F32),
        scratch_shapes=[pltpu.VMEM((tm, D_MODEL), BF16)],
        compiler_params=_params("arbitrary"), name="attn_out",
    )(y_a, y_b, w_a16, w_b16, proj, proj, proj, proj, w_out16, x2)


def _res_mm_kernel(a_ref, w_ref, x_ref, o_ref):
    o_ref[...] = x_ref[...] + _dot(a_ref[...], w_ref[...])


def _res_mm(a, w, layer, x2, tn):
    n, k = a.shape
    cols = w.shape[2]
    tm = min(MM_TM, n)
    return pl.pallas_call(
        _res_mm_kernel, grid=(n // tm, cols // tn),
        in_specs=[
            pl.BlockSpec((tm, k), lambda i, j: (i, 0)),
            pl.BlockSpec((None, k, tn), lambda i, j: (layer, 0, j)),
            pl.BlockSpec((tm, tn), lambda i, j: (i, j)),
        ],
        out_specs=pl.BlockSpec((tm, tn), lambda i, j: (i, j)),
        out_shape=jax.ShapeDtypeStruct((n, cols), F32),
        compiler_params=_params("arbitrary", "arbitrary"), name="res_mm",
    )(a, w, x2)


def _ffn_in_kernel(tiles_per_seq, x_ref, g_ref, wg_ref, wu_ref, cw_ref, cb_ref, o_ref,
                   h_ref, cs_ref, tail_ref):
    i = pl.program_id(0)
    j = pl.program_id(1)
    tm = x_ref.shape[0]

    seq_start = (i % tiles_per_seq) == 0

    @pl.when(seq_start)
    def _():
        cs_ref[0:8, :] = jnp.zeros((8, cs_ref.shape[1]), F32)

    @pl.when(jnp.logical_not(seq_start))
    def _():
        cs_ref[0:8, :] = tail_ref[j]

    rc = min(FFN_ROWS, tm)

    def epilogue(r, up):
        y = cb_ref[...] + cw_ref[0:1, :] * cs_ref[6 + r * rc:6 + (r + 1) * rc, :]
        for t in range(1, FFN_CONV):
            y = y + cw_ref[t:t + 1, :] * cs_ref[6 + t + r * rc:6 + t + (r + 1) * rc, :]
        o_ref[r * rc:(r + 1) * rc, :] = (y * _sigmoid(y) * up).astype(o_ref.dtype)

    def step(normalise):
        def dots(r):
            if normalise:
                _rms_block(x_ref, g_ref, h_ref, r * rc, rc)
            h = h_ref[r * rc:(r + 1) * rc, :]
            cs_ref[8 + r * rc:8 + (r + 1) * rc, :] = _dot(h, wg_ref[...])
            return _dot(h, wu_ref[...])

        up_prev = dots(0)
        for r in range(1, tm // rc):
            up_next = dots(r)
            epilogue(r - 1, up_prev)
            up_prev = up_next
        tail_ref[j] = cs_ref[tm:tm + 8, :]
        epilogue(tm // rc - 1, up_prev)

    @pl.when(j == 0)
    def _():
        step(True)

    @pl.when(j > 0)
    def _():
        step(False)


def _ffn_in(x2, gain, w_in16, layer, conv_w, conv_b, seq):
    n, d = x2.shape
    tm = min(MM_TM, seq)
    nj = D_FF // FFN_TN
    return pl.pallas_call(
        functools.partial(_ffn_in_kernel, seq // tm),
        grid=(n // tm, nj),
        in_specs=[
            pl.BlockSpec((tm, d), lambda i, j: (i, 0)),
            pl.BlockSpec((1, d), lambda i, j: (0, 0)),
            pl.BlockSpec((None, d, FFN_TN), lambda i, j: (layer, 0, j)),
            pl.BlockSpec((None, d, FFN_TN), lambda i, j: (layer, 0, nj + j)),
            pl.BlockSpec((8, FFN_TN), lambda i, j: (0, j)),
            pl.BlockSpec((1, FFN_TN), lambda i, j: (0, j)),
        ],
        out_specs=pl.BlockSpec((tm, FFN_TN), lambda i, j: (i, j)),
        out_shape=jax.ShapeDtypeStruct((n, D_FF), BF16),
        scratch_shapes=[
            pltpu.VMEM((tm, d), BF16),
            pltpu.VMEM((tm + 8, FFN_TN), F32),
            pltpu.VMEM((nj, 8, FFN_TN), F32),
        ],
        compiler_params=_params("arbitrary", "arbitrary"), name="ffn_in",
    )(x2, gain, w_in16, w_in16, conv_w, conv_b)


def _ffn(x2, gain, w_ffn_in16, conv_w, conv_b, w_down16, layer, seq):
    act = _ffn_in(x2, gain[None, :], w_ffn_in16, layer, _pad_rows(conv_w, 8), conv_b[None, :], seq)
    return _res_mm(act, w_down16, layer, x2, FFN_TN)


def _pad_rows(w, rows):
    return jnp.pad(w, ((0, rows - w.shape[0]), (0, 0)))


def _lane_row(v, offset):
    return jnp.zeros((1, SMALL_COLS), F32).at[0, offset:offset + v.shape[0]].set(v.astype(F32))


def kernel(x, attn_norm, w_in, moba_q_norm, moba_k_norm, dn_conv, dn_a_log, dn_dt_bias, dn_out_norm,
           w_branch_a, w_branch_b, w_out, ffn_norm, w_ffn_in, ffn_conv, ffn_conv_bias, w_ffn_down):
    batch, seq, d = x.shape
    depth = w_in.shape[0]
    n = batch * seq
    assert d == D_MODEL and seq % MOBA_BLOCK == 0 and seq % DN_TILE == 0 and n % min(MM_TM, n) == 0
    small_w = 2 * DN_HEADS
    x2 = x.reshape(n, d)
    w_in16 = w_in.astype(BF16)
    w_gates16 = w_in16[:, :, MAIN_COLS + small_w:]
    w_small16 = jnp.pad(w_in16[:, :, MAIN_COLS:MAIN_COLS + small_w], ((0, 0), (0, 0), (0, SMALL_COLS - small_w)))
    w_a16 = w_branch_a.astype(BF16)
    w_b16 = w_branch_b.astype(BF16)
    w_out16 = w_out.astype(BF16)
    w_ffn_in16 = w_ffn_in.astype(BF16)
    w_ffn_down16 = w_ffn_down.astype(BF16)
    for l in range(depth):
        proj, small = _in_proj(x2, attn_norm[l][None, :], w_in16, w_gates16, w_small16, l)
        y_a = _moba(proj, moba_q_norm[l][None, :], moba_k_norm[l][None, :], batch, seq)
        y_b = _deltanet(proj, small, _pad_rows(dn_conv[l], 8), _lane_row(dn_a_log[l], DN_HEADS),
                        _lane_row(dn_dt_bias[l], DN_HEADS), dn_out_norm[l][None, :], batch, seq)
        x2 = _attn_out(y_a, y_b, w_a16, w_b16, proj, w_out16, l, x2)
        x2 = _ffn(x2, ffn_norm[l], w_ffn_in16, ffn_conv[l], ffn_conv_bias[l], w_ffn_down16, l, seq)
    return x2.reshape(batch, seq, d)
```

```python
import functools

import jax
import jax.numpy as jnp
from jax import lax
from jax.experimental import pallas as pl
from jax.experimental.pallas import tpu as pltpu

F32 = jnp.float32
BF16 = jnp.bfloat16

D_MODEL = 2048
HEAD_DIM = 128
MOBA_HEADS = 8
MOBA_WIDTH = MOBA_HEADS * HEAD_DIM
MOBA_BLOCK = 256
MOBA_TOPK = 3
DN_HEADS = 8
DN_WIDTH = DN_HEADS * HEAD_DIM
DN_CONV = 4
D_FF = 5632
FFN_CONV = 3
NORM_EPS = 1e-6

MAIN_COLS = 3 * MOBA_WIDTH + 4 * DN_WIDTH
PROJ_COLS = MAIN_COLS + 2 * D_MODEL
SMALL_COLS = 128

VMEM_LIMIT_BYTES = 56 * 1024 * 1024

MM_TM = 1024
MM_TN = 1024
NORM_ROWS = 128
MM_ROWS = 256
ATTN_OUT_TM = 512
MOBA_GROUP = 8
MOBA_KV = 2
MOBA_VROWS = HEAD_DIM + 16
LOG2E = 1.4426950408889634
DN_TILE = 128
DN_SEQS = 2
DN_HIST = 16
FFN_TN = 512
FFN_ROWS = 512
NEG = -1e30
NEG16 = -1e30


def _sigmoid(x):
    return 1.0 / (1.0 + jnp.exp(-x))


def _dot(a, b):
    return jnp.dot(a, b, preferred_element_type=F32)


def _dot_nt(a, b, precision=None):
    return lax.dot_general(a, b, (((1,), (1,)), ((), ())), precision=precision,
                           preferred_element_type=F32)


def _dot_tn(a, b):
    return lax.dot_general(a, b, (((0,), (0,)), ((), ())), preferred_element_type=F32)


def _params(*sem):
    return pltpu.CompilerParams(dimension_semantics=sem, vmem_limit_bytes=VMEM_LIMIT_BYTES)


def _rms_block(x_ref, g_ref, h_ref, row0, rows):
    for r in range(row0, row0 + rows, NORM_ROWS):
        x = x_ref[r:r + NORM_ROWS, :]
        ms = jnp.mean(x * x, axis=-1, keepdims=True)
        h_ref[r:r + NORM_ROWS, :] = (x * lax.rsqrt(ms + NORM_EPS) * g_ref[...]).astype(h_ref.dtype)


def _in_proj_kernel(n_main, x_ref, g_ref, wm_ref, wg_ref, ws_ref, o_ref, s_ref, h_ref):
    j = pl.program_id(1)
    tm = x_ref.shape[0]
    rc = min(MM_ROWS, tm)

    @pl.when(j == 0)
    def _():
        for r in range(0, tm, rc):
            _rms_block(x_ref, g_ref, h_ref, r, rc)
            h = h_ref[r:r + rc, :]
            s_ref[r:r + rc, :] = _dot(h, ws_ref[...])
            o_ref[r:r + rc, :] = _dot(h, wm_ref[...]).astype(o_ref.dtype)

    @pl.when(jnp.logical_and(j > 0, j < n_main))
    def _():
        o_ref[...] = _dot(h_ref[...], wm_ref[...]).astype(o_ref.dtype)

    @pl.when(j >= n_main)
    def _():
        o_ref[...] = _dot(h_ref[...], wg_ref[...]).astype(o_ref.dtype)


def _in_proj(x2, gain, w_in16, w_gates16, w_small16, layer):
    n, d = x2.shape
    tm = min(MM_TM, n)
    n_main = MAIN_COLS // MM_TN
    n_gate = w_gates16.shape[2] // MM_TN
    return pl.pallas_call(
        functools.partial(_in_proj_kernel, n_main),
        grid=(n // tm, n_main + n_gate),
        in_specs=[
            pl.BlockSpec((tm, d), lambda i, j: (i, 0)),
            pl.BlockSpec((1, d), lambda i, j: (0, 0)),
            pl.BlockSpec((None, d, MM_TN), lambda i, j: (layer, 0, jnp.minimum(j, n_main - 1))),
            pl.BlockSpec((None, d, MM_TN), lambda i, j: (layer, 0, jnp.maximum(j - n_main, 0))),
            pl.BlockSpec((None, d, SMALL_COLS), lambda i, j: (layer, 0, 0)),
        ],
        out_specs=[pl.BlockSpec((tm, MM_TN), lambda i, j: (i, j)),
                   pl.BlockSpec((tm, SMALL_COLS), lambda i, j: (i, 0))],
        out_shape=[jax.ShapeDtypeStruct((n, PROJ_COLS), BF16),
                   jax.ShapeDtypeStruct((n, SMALL_COLS), F32)],
        scratch_shapes=[pltpu.VMEM((tm, d), BF16)],
        compiler_params=_params("arbitrary", "arbitrary"), name="in_proj",
    )(x2, gain, w_in16, w_gates16, w_small16)


def _moba_kernel(q_ref, k_ref, v_ref, qg_ref, kg_ref, o_ref, kn_ref, vt_ref, km_ref, sel_ref):
    qi = pl.program_id(2)
    hp, nb = kn_ref.shape[0], kn_ref.shape[1]
    blk = MOBA_BLOCK
    heads = range(hp)

    def cols(h):
        return slice(h * HEAD_DIM, (h + 1) * HEAD_DIM)

    @pl.when(qi == 0)
    def _():
        tail_rows = lax.broadcasted_iota(jnp.int32, (MOBA_VROWS - HEAD_DIM, blk), 0)
        ones_tail = jnp.where(tail_rows == 0, 1.0, 0.0).astype(BF16)

        def prep(n, carry):
            r = pl.multiple_of(n * blk, blk)
            for h in heads:
                kb = k_ref[pl.ds(r, blk), cols(h)].astype(F32)
                ms = jnp.mean(kb * kb, axis=-1, keepdims=True)
                kn = kb * lax.rsqrt(ms + NORM_EPS) * kg_ref[...]
                kn_ref[h, n] = kn.astype(BF16)
                km_ref[h, pl.ds(n, 1), :] = jnp.mean(kn, axis=0, keepdims=True)
                vt_ref[h, n, 0:HEAD_DIM, :] = v_ref[pl.ds(r, blk), cols(h)].astype(F32).T.astype(BF16)
                vt_ref[h, n, HEAD_DIM:MOBA_VROWS, :] = ones_tail
            return carry

        lax.fori_loop(0, nb, prep, 0)

    q = [q_ref[:, cols(h)].astype(F32) for h in heads]
    qn = [t * lax.rsqrt(jnp.mean(t * t, axis=-1, keepdims=True) + NORM_EPS) * qg_ref[...] for t in q]

    gate = [_dot_nt(km_ref[h], qn[h], precision=lax.Precision.HIGHEST) for h in heads]
    rows = lax.broadcasted_iota(jnp.int32, (nb, blk), 0).astype(F32)
    past = rows < qi.astype(F32)
    gate = [jnp.where(past, g, NEG) for g in gate]
    sel = [jnp.zeros((nb, blk), F32) for _ in heads]
    for _ in range(MOBA_TOPK):
        mx = [jnp.max(g, axis=0, keepdims=True) for g in gate]
        idx = [jnp.min(jnp.where(gate[h] == mx[h], rows, float(nb)), axis=0, keepdims=True) for h in heads]
        pick = [jnp.logical_and(rows == idx[h], mx[h] > 0.5 * NEG) for h in heads]
        sel = [jnp.where(pick[h], 1.0, sel[h]) for h in heads]
        gate = [jnp.where(pick[h], NEG, gate[h]) for h in heads]
    for h in heads:
        sel_ref[h] = sel[h]

    qb = [(t * (HEAD_DIM ** -0.5 * LOG2E)).astype(BF16) for t in qn]

    s = [_dot_nt(kn_ref[h, qi], qb[h]) for h in heads]
    kidx = lax.broadcasted_iota(jnp.int32, (blk, blk), 0)
    qidx = lax.broadcasted_iota(jnp.int32, (blk, blk), 1)
    causal = kidx <= qidx
    s = [jnp.where(causal, t, NEG) for t in s]
    m0 = [jnp.max(t, axis=0, keepdims=True) for t in s]
    p = [jnp.exp2(s[h] - m0[h]) for h in heads]
    acc0 = [_dot(vt_ref[h, qi], p[h].astype(BF16)) for h in heads]

    kvb = MOBA_KV
    units = [(h, u) for h in heads for u in range(kvb)]

    def body(t, carry):
        m, acc = carry
        js = [jnp.minimum(t * kvb + u, nb - 1) for u in range(kvb)]
        dj = {(h, u): (_dot_nt(kn_ref[h, js[u]], qb[h]) - m[h]).astype(BF16) for h, u in units}
        keep = {(h, u): (sel_ref[h, pl.ds(js[u], 1), :] > 0.0) for h, u in units}
        dj = {(h, u): jnp.where(keep[h, u], dj[h, u], jnp.full_like(dj[h, u], NEG16)) for h, u in units}
        bmax = {(h, u): jnp.max(dj[h, u], axis=0, keepdims=True) for h, u in units}
        up = []
        for h in heads:
            uh = jnp.zeros_like(bmax[h, 0])
            for u in range(kvb):
                uh = jnp.maximum(uh, bmax[h, u])
            up.append(uh)
        up32 = [t.astype(F32) for t in up]
        m_new = [m[h] + up32[h] for h in heads]
        alpha = [jnp.exp2(-up32[h]) for h in heads]
        pj = {(h, u): jnp.exp2(dj[h, u] - up[h]) for h, u in units}
        pv = {(h, u): _dot(vt_ref[h, js[u]], pj[h, u]) for h, u in units}
        acc_new = []
        for h in heads:
            ah = alpha[h] * acc[h]
            for u in range(kvb):
                ah = ah + pv[h, u]
            acc_new.append(ah)
        return m_new, acc_new

    _, acc = lax.fori_loop(0, (qi + (kvb - 1)) // kvb, body, (m0, acc0))
    for h in heads:
        out = acc[h][0:HEAD_DIM, :] / acc[h][HEAD_DIM:HEAD_DIM + 1, :]
        o_ref[:, cols(h)] = out.T.astype(o_ref.dtype)


def _moba(proj, q_gain, k_gain, batch, seq):
    n = batch * seq
    nb = seq // MOBA_BLOCK
    hp = MOBA_GROUP
    gw = hp * HEAD_DIM
    groups = MOBA_HEADS // hp
    return pl.pallas_call(
        _moba_kernel, grid=(batch, groups, nb),
        in_specs=[
            pl.BlockSpec((MOBA_BLOCK, gw), lambda b, g, i: (b * nb + i, g)),
            pl.BlockSpec((seq, gw), lambda b, g, i: (b, groups + g), pipeline_mode=pl.Buffered(1)),
            pl.BlockSpec((seq, gw), lambda b, g, i: (b, 2 * groups + g), pipeline_mode=pl.Buffered(1)),
            pl.BlockSpec((1, HEAD_DIM), lambda b, g, i: (0, 0)),
            pl.BlockSpec((1, HEAD_DIM), lambda b, g, i: (0, 0)),
        ],
        out_specs=pl.BlockSpec((MOBA_BLOCK, gw), lambda b, g, i: (b * nb + i, g)),
        out_shape=jax.ShapeDtypeStruct((n, MOBA_WIDTH), BF16),
        scratch_shapes=[
            pltpu.VMEM((hp, nb, MOBA_BLOCK, HEAD_DIM), BF16),
            pltpu.VMEM((hp, nb, MOBA_VROWS, MOBA_BLOCK), BF16),
            pltpu.VMEM((hp, nb, HEAD_DIM), F32),
            pltpu.VMEM((hp, nb, MOBA_BLOCK), F32),
        ],
        compiler_params=_params("arbitrary", "arbitrary", "arbitrary"), name="moba",
    )(proj, proj, proj, q_gain, k_gain)


def _dot_split(a16, b):
    hi = b.astype(BF16)
    lo = (b - hi.astype(F32)).astype(BF16)
    return _dot(a16, hi) + _dot(a16, lo)


def _dn_kernel(qkv_ref, z_ref, sm_ref, cw_ref, al_ref, dtb_ref, og_ref, o_ref, xe_ref, st_ref):
    c = DN_TILE
    ns, _, width = qkv_ref.shape
    seqs = range(ns)

    @pl.when(pl.program_id(1) == 0)
    def _():
        xe_ref[:, 0:DN_HIST, :] = jnp.zeros((ns, DN_HIST, width), BF16)
        st_ref[...] = jnp.zeros(st_ref.shape, F32)

    ri = lax.broadcasted_iota(jnp.int32, (c, c), 0)
    ci = lax.broadcasted_iota(jnp.int32, (c, c), 1)
    tril = ri >= ci
    strict = ri > ci
    eye = jnp.where(ri == ci, 1.0, 0.0)
    lower_blocks = []
    s = 1
    while s < c:
        same_2s = ((ri ^ ci) & ~(2 * s - 1)) == 0
        lower_blocks.append(jnp.logical_and(same_2s, jnp.logical_and((ri & s) != 0, (ci & s) == 0)))
        s *= 2
    tril16 = jnp.where(tril, 1.0, 0.0).astype(BF16)
    sr = lax.broadcasted_iota(jnp.int32, (c, DN_HIST + c), 0)
    sc = lax.broadcasted_iota(jnp.int32, (c, DN_HIST + c), 1)
    shifts = [jnp.where(sc == sr + (DN_HIST - d), 1.0, 0.0).astype(BF16) for d in range(1, DN_CONV)]


    ys = []
    for b in seqs:
        xe_ref[b, DN_HIST:DN_HIST + c, :] = qkv_ref[b]
        xe = xe_ref[b]
        y = cw_ref[DN_CONV - 1:DN_CONV, :] * qkv_ref[b].astype(F32)
        for d in range(1, DN_CONV):
            y = y + cw_ref[DN_CONV - 1 - d:DN_CONV - d, :] * _dot(shifts[d - 1], xe)
        xe_ref[b, 0:DN_HIST, :] = xe_ref[b, c:c + DN_HIST, :]
        ys.append(y * _sigmoid(y))

    sm = [sm_ref[b] for b in seqs]
    beta_all = [_sigmoid(t) for t in sm]
    xg = [t + dtb_ref[...] for t in sm]
    softplus = [jnp.maximum(t, 0.0) + jnp.log(1.0 + jnp.exp(-jnp.abs(t))) for t in xg]
    g_all = [-jnp.exp(al_ref[...]) * t for t in softplus]
    gcum = [_dot_split(tril16, t) for t in g_all]
    gcum_t = [t.T for t in gcum]

    units = [(b, h) for b in seqs for h in range(DN_HEADS)]

    def head_cols(b, base, h):
        return ys[b][:, base + h * HEAD_DIM:base + (h + 1) * HEAD_DIM]

    q = {u: head_cols(u[0], 0, u[1]) for u in units}
    k = {u: head_cols(u[0], DN_WIDTH, u[1]) for u in units}
    v = {u: head_cols(u[0], 2 * DN_WIDTH, u[1]) for u in units}
    q = {u: t * lax.rsqrt(jnp.sum(t * t, axis=-1, keepdims=True) + NORM_EPS) * (HEAD_DIM ** -0.5)
         for u, t in q.items()}
    k = {u: t * lax.rsqrt(jnp.sum(t * t, axis=-1, keepdims=True) + NORM_EPS) for u, t in k.items()}
    beta = {(b, h): beta_all[b][:, h:h + 1] for b, h in units}
    g_col = {(b, h): gcum[b][:, DN_HEADS + h:DN_HEADS + h + 1] for b, h in units}
    g_row = {(b, h): gcum_t[b][DN_HEADS + h:DN_HEADS + h + 1, :] for b, h in units}
    g_last = {u: t[c - 1:c, :] for u, t in g_col.items()}
    decay = {u: jnp.where(tril, jnp.exp(jnp.minimum(g_col[u] - g_row[u], 0.0)), 0.0) for u in units}
    e_g = {u: jnp.exp(t) for u, t in g_col.items()}
    kb = {u: k[u] * beta[u] for u in units}
    k16 = {u: t.astype(BF16) for u, t in k.items()}
    nmat = {u: jnp.where(strict, _dot_nt(kb[u].astype(BF16), k16[u]) * decay[u], 0.0) for u in units}
    attn = {u: (_dot_nt(q[u].astype(BF16), k16[u]) * decay[u]).astype(BF16) for u in units}

    rhs = {u: jnp.concatenate([v[u] * beta[u], kb[u] * e_g[u]], axis=1).astype(BF16) for u in units}
    n16 = {u: t.astype(BF16) for u, t in nmat.items()}
    tinv = {u: eye - jnp.where(lower_blocks[0], t, 0.0) for u, t in nmat.items()}
    for msk in lower_blocks[1:]:
        t16 = {u: t.astype(BF16) for u, t in tinv.items()}
        cm = {u: jnp.where(msk, t, jnp.zeros_like(t)) for u, t in n16.items()}
        p = {u: _dot(cm[u], t16[u]).astype(BF16) for u in units}
        tinv = {u: tinv[u] - _dot(t16[u], p[u]) for u in units}
    sol = {u: _dot(tinv[u].astype(BF16), rhs[u]) for u in units}

    state = {(b, h): st_ref[b, h] for b, h in units}
    s16 = {u: t.astype(BF16) for u, t in state.items()}
    v16 = {u: (sol[u][:, :HEAD_DIM] - _dot(sol[u][:, HEAD_DIM:].astype(BF16), s16[u])).astype(BF16)
           for u in units}
    o = {u: _dot((q[u] * e_g[u]).astype(BF16), s16[u]) + _dot(attn[u], v16[u]) for u in units}
    kd = {u: (k[u] * jnp.exp(g_last[u] - g_col[u])).astype(BF16) for u in units}
    for b, h in units:
        st_ref[b, h] = state[b, h] * jnp.exp(g_last[b, h]) + _dot_tn(kd[b, h], v16[b, h])
    for b, h in units:
        ou = o[b, h]
        on = ou * lax.rsqrt(jnp.mean(ou * ou, axis=-1, keepdims=True) + NORM_EPS) * og_ref[...]
        zh = z_ref[b, :, h * HEAD_DIM:(h + 1) * HEAD_DIM].astype(F32)
        o_ref[b, :, h * HEAD_DIM:(h + 1) * HEAD_DIM] = (on * (zh * _sigmoid(zh))).astype(o_ref.dtype)


def _deltanet(proj, small, conv_w, a_log_row, dt_bias_row, out_gain, batch, seq):
    nt = seq // DN_TILE
    ns = DN_SEQS if batch % DN_SEQS == 0 else 1
    qkv_w = 3 * DN_WIDTH
    qkv_blk = (3 * MOBA_WIDTH) // qkv_w
    z_blk = (3 * MOBA_WIDTH + qkv_w) // DN_WIDTH
    proj3 = proj.reshape(batch, seq, proj.shape[1])
    small3 = small.reshape(batch, seq, SMALL_COLS)
    y_b = pl.pallas_call(
        _dn_kernel, grid=(batch // ns, nt),
        in_specs=[
            pl.BlockSpec((ns, DN_TILE, qkv_w), lambda g, t: (g, t, qkv_blk)),
            pl.BlockSpec((ns, DN_TILE, DN_WIDTH), lambda g, t: (g, t, z_blk)),
            pl.BlockSpec((ns, DN_TILE, SMALL_COLS), lambda g, t: (g, t, 0)),
            pl.BlockSpec((8, qkv_w), lambda g, t: (0, 0)),
            pl.BlockSpec((1, SMALL_COLS), lambda g, t: (0, 0)),
            pl.BlockSpec((1, SMALL_COLS), lambda g, t: (0, 0)),
            pl.BlockSpec((1, HEAD_DIM), lambda g, t: (0, 0)),
        ],
        out_specs=pl.BlockSpec((ns, DN_TILE, DN_WIDTH), lambda g, t: (g, t, 0)),
        out_shape=jax.ShapeDtypeStruct((batch, seq, DN_WIDTH), BF16),
        scratch_shapes=[
            pltpu.VMEM((ns, DN_HIST + DN_TILE, qkv_w), BF16),
            pltpu.VMEM((ns, DN_HEADS, HEAD_DIM, HEAD_DIM), F32),
        ],
        compiler_params=_params("arbitrary", "arbitrary"), name="deltanet",
    )(proj3, proj3, small3, conv_w, a_log_row, dt_bias_row, out_gain)
    return y_b.reshape(batch * seq, DN_WIDTH)


def _attn_out_kernel(ya_ref, yb_ref, wa_ref, wb_ref, ga0_ref, ga1_ref, gb0_ref, gb1_ref, wo_ref, x_ref,
                     o_ref, mg_ref):
    tm = ya_ref.shape[0]
    rc = min(MM_ROWS, tm)
    half = D_MODEL // 2
    gates = ((ga0_ref, gb0_ref), (ga1_ref, gb1_ref))

    def dots(r, c):
        rows = slice(r * rc, (r + 1) * rc)
        cols = slice(c * half, (c + 1) * half)
        return _dot(ya_ref[rows, :], wa_ref[:, cols]), _dot(yb_ref[rows, :], wb_ref[:, cols])

    def combine(r, c, ab):
        rows = slice(r * rc, (r + 1) * rc)
        ga = _sigmoid(gates[c][0][rows, :].astype(F32))
        gb = _sigmoid(gates[c][1][rows, :].astype(F32))
        mg_ref[rows, c * half:(c + 1) * half] = (ga * ab[0] + gb * ab[1]).astype(mg_ref.dtype)

    def project(r):
        rows = slice(r * rc, (r + 1) * rc)
        o_ref[rows, :] = x_ref[rows, :] + _dot(mg_ref[rows, :], wo_ref[...])

    units = [(r, c) for r in range(tm // rc) for c in range(2)]
    pending = None
    for r, c in units:
        ab = dots(r, c)
        if pending is not None:
            combine(*pending)
            if pending[1] == 1:
                project(pending[0])
        pending = (r, c, ab)
    combine(*pending)
    project(pending[0])


def _attn_out(y_a, y_b, w_a16, w_b16, proj, w_out16, layer, x2):
    n = y_a.shape[0]
    tm = min(ATTN_OUT_TM, n)
    half = D_MODEL // 2
    ga_blk = MAIN_COLS // half
    gb_blk = ga_blk + 2
    resident = dict(pipeline_mode=pl.Buffered(1))
    return pl.pallas_call(
        _attn_out_kernel, grid=(n // tm,),
        in_specs=[
            pl.BlockSpec((tm, MOBA_WIDTH), lambda i: (i, 0)),
            pl.BlockSpec((tm, DN_WIDTH), lambda i: (i, 0)),
            pl.BlockSpec((None, MOBA_WIDTH, D_MODEL), lambda i: (layer, 0, 0), **resident),
            pl.BlockSpec((None, DN_WIDTH, D_MODEL), lambda i: (layer, 0, 0), **resident),
            pl.BlockSpec((tm, half), lambda i: (i, ga_blk)),
            pl.BlockSpec((tm, half), lambda i: (i, ga_blk + 1)),
            pl.BlockSpec((tm, half), lambda i: (i, gb_blk)),
            pl.BlockSpec((tm, half), lambda i: (i, gb_blk + 1)),
            pl.BlockSpec((None, D_MODEL, D_MODEL), lambda i: (layer, 0, 0), **resident),
            pl.BlockSpec((tm, D_MODEL), lambda i: (i, 0)),
        ],
        out_specs=pl.BlockSpec((tm, D_MODEL), lambda i: (i, 0)),
        out_shape=jax.ShapeDtypeStruct((n, D_MODEL), F32),
        scratch_shapes=[pltpu.VMEM((tm, D_MODEL), BF16)],
        compiler_params=_params("arbitrary"), name="attn_out",
    )(y_a, y_b, w_a16, w_b16, proj, proj, proj, proj, w_out16, x2)


def _res_mm_kernel(a_ref, w_ref, x_ref, o_ref):
    o_ref[...] = x_ref[...] + _dot(a_ref[...], w_ref[...])


def _res_mm(a, w, layer, x2, tn):
    n, k = a.shape
    cols = w.shape[2]
    tm = min(MM_TM, n)
    return pl.pallas_call(
        _res_mm_kernel, grid=(n // tm, cols // tn),
        in_specs=[
            pl.BlockSpec((tm, k), lambda i, j: (i, 0)),
            pl.BlockSpec((None, k, tn), lambda i, j: (layer, 0, j)),
            pl.BlockSpec((tm, tn), lambda i, j: (i, j)),
        ],
        out_specs=pl.BlockSpec((tm, tn), lambda i, j: (i, j)),
        out_shape=jax.ShapeDtypeStruct((n, cols), F32),
        compiler_params=_params("arbitrary", "arbitrary"), name="res_mm",
    )(a, w, x2)


def _ffn_in_kernel(tiles_per_seq, x_ref, g_ref, wg_ref, wu_ref, cw_ref, cb_ref, o_ref,
                   h_ref, cs_ref, tail_ref):
    i = pl.program_id(0)
    j = pl.program_id(1)
    tm = x_ref.shape[0]

    seq_start = (i % tiles_per_seq) == 0

    @pl.when(seq_start)
    def _():
        cs_ref[0:8, :] = jnp.zeros((8, cs_ref.shape[1]), F32)

    @pl.when(jnp.logical_not(seq_start))
    def _():
        cs_ref[0:8, :] = tail_ref[j]

    rc = min(FFN_ROWS, tm)

    def activation(r):
        y = cb_ref[...] + cw_ref[0:1, :] * cs_ref[6 + r * rc:6 + (r + 1) * rc, :]
        for t in range(1, FFN_CONV):
            y = y + cw_ref[t:t + 1, :] * cs_ref[6 + t + r * rc:6 + t + (r + 1) * rc, :]
        return y * _sigmoid(y)

    def step(normalise):
        for r in range(tm // rc):
            if normalise:
                _rms_block(x_ref, g_ref, h_ref, r * rc, rc)
            h = h_ref[r * rc:(r + 1) * rc, :]
            cs_ref[8 + r * rc:8 + (r + 1) * rc, :] = _dot(h, wg_ref[...])
            act = activation(r)
            o_ref[r * rc:(r + 1) * rc, :] = (act * _dot(h, wu_ref[...])).astype(o_ref.dtype)
        tail_ref[j] = cs_ref[tm:tm + 8, :]

    @pl.when(j == 0)
    def _():
        step(True)

    @pl.when(j > 0)
    def _():
        step(False)


def _ffn_in(x2, gain, w_in16, layer, conv_w, conv_b, seq):
    n, d = x2.shape
    tm = min(MM_TM, seq)
    nj = D_FF // FFN_TN
    return pl.pallas_call(
        functools.partial(_ffn_in_kernel, seq // tm),
        grid=(n // tm, nj),
        in_specs=[
            pl.BlockSpec((tm, d), lambda i, j: (i, 0)),
            pl.BlockSpec((1, d), lambda i, j: (0, 0)),
            pl.BlockSpec((None, d, FFN_TN), lambda i, j: (layer, 0, j)),
            pl.BlockSpec((None, d, FFN_TN), lambda i, j: (layer, 0, nj + j)),
            pl.BlockSpec((8, FFN_TN), lambda i, j: (0, j)),
            pl.BlockSpec((1, FFN_TN), lambda i, j: (0, j)),
        ],
        out_specs=pl.BlockSpec((tm, FFN_TN), lambda i, j: (i, j)),
        out_shape=jax.ShapeDtypeStruct((n, D_FF), BF16),
        scratch_shapes=[
            pltpu.VMEM((tm, d), BF16),
            pltpu.VMEM((tm + 8, FFN_TN), F32),
            pltpu.VMEM((nj, 8, FFN_TN), F32),
        ],
        compiler_params=_params("arbitrary", "arbitrary"), name="ffn_in",
    )(x2, gain, w_in16, w_in16, conv_w, conv_b)


def _ffn(x2, gain, w_ffn_in16, conv_w, conv_b, w_down16, layer, seq):
    act = _ffn_in(x2, gain[None, :], w_ffn_in16, layer, _pad_rows(conv_w, 8), conv_b[None, :], seq)
    return _res_mm(act, w_down16, layer, x2, FFN_TN)


def _pad_rows(w, rows):
    return jnp.pad(w, ((0, rows - w.shape[0]), (0, 0)))


def _lane_row(v, offset):
    return jnp.zeros((1, SMALL_COLS), F32).at[0, offset:offset + v.shape[0]].set(v.astype(F32))


def kernel(x, attn_norm, w_in, moba_q_norm, moba_k_norm, dn_conv, dn_a_log, dn_dt_bias, dn_out_norm,
           w_branch_a, w_branch_b, w_out, ffn_norm, w_ffn_in, ffn_conv, ffn_conv_bias, w_ffn_down):
    batch, seq, d = x.shape
    depth = w_in.shape[0]
    n = batch * seq
    assert d == D_MODEL and seq % MOBA_BLOCK == 0 and seq % DN_TILE == 0 and n % min(MM_TM, n) == 0
    small_w = 2 * DN_HEADS
    x2 = x.reshape(n, d)
    w_in16 = w_in.astype(BF16)
    w_gates16 = w_in16[:, :, MAIN_COLS + small_w:]
    w_small16 = jnp.pad(w_in16[:, :, MAIN_COLS:MAIN_COLS + small_w], ((0, 0), (0, 0), (0, SMALL_COLS - small_w)))
    w_a16 = w_branch_a.astype(BF16)
    w_b16 = w_branch_b.astype(BF16)
    w_out16 = w_out.astype(BF16)
    w_ffn_in16 = w_ffn_in.astype(BF16)
    w_ffn_down16 = w_ffn_down.astype(BF16)
    for l in range(depth):
        proj, small = _in_proj(x2, attn_norm[l][None, :], w_in16, w_gates16, w_small16, l)
        y_a = _moba(proj, moba_q_norm[l][None, :], moba_k_norm[l][None, :], batch, seq)
        y_b = _deltanet(proj, small, _pad_rows(dn_conv[l], 8), _lane_row(dn_a_log[l], DN_HEADS),
                        _lane_row(dn_dt_bias[l], DN_HEADS), dn_out_norm[l][None, :], batch, seq)
        x2 = _attn_out(y_a, y_b, w_a16, w_b16, proj, w_out16, l, x2)
        x2 = _ffn(x2, ffn_norm[l], w_ffn_in16, ffn_conv[l], ffn_conv_bias[l], w_ffn_down16, l, seq)
    return x2.reshape(batch, seq, d)
```

```python
import functools

import jax
import jax.numpy as jnp
from jax import lax
from jax.experimental import pallas as pl
from jax.experimental.pallas import tpu as pltpu

F32 = jnp.float32
BF16 = jnp.bfloat16

D_MODEL = 2048
HEAD_DIM = 128
MOBA_HEADS = 8
MOBA_WIDTH = MOBA_HEADS * HEAD_DIM
MOBA_BLOCK = 256
MOBA_TOPK = 3
DN_HEADS = 8
DN_WIDTH = DN_HEADS * HEAD_DIM
DN_CONV = 4
D_FF = 5632
FFN_CONV = 3
NORM_EPS = 1e-6

MAIN_COLS = 3 * MOBA_WIDTH + 4 * DN_WIDTH
PROJ_COLS = MAIN_COLS + 2 * D_MODEL
SMALL_COLS = 128

VMEM_LIMIT_BYTES = 56 * 1024 * 1024

MM_TM = 1024
MM_TN = 1024
NORM_ROWS = 128
MM_ROWS = 256
ATTN_OUT_TM = 512
MOBA_GROUP = 8
MOBA_KV = 2
MOBA_VROWS = HEAD_DIM + 16
LOG2E = 1.4426950408889634
DN_TILE = 128
DN_SEQS = 2
DN_HIST = 16
FFN_TN = 512
FFN_ROWS = 512
NEG = -1e30
NEG16 = -1e30


def _sigmoid(x):
    return 1.0 / (1.0 + jnp.exp(-x))


def _dot(a, b):
    return jnp.dot(a, b, preferred_element_type=F32)


def _dot_nt(a, b, precision=None):
    return lax.dot_general(a, b, (((1,), (1,)), ((), ())), precision=precision,
                           preferred_element_type=F32)


def _dot_tn(a, b):
    return lax.dot_general(a, b, (((0,), (0,)), ((), ())), preferred_element_type=F32)


def _params(*sem):
    return pltpu.CompilerParams(dimension_semantics=sem, vmem_limit_bytes=VMEM_LIMIT_BYTES)


def _rms_block(x_ref, g_ref, h_ref, row0, rows):
    for r in range(row0, row0 + rows, NORM_ROWS):
        x = x_ref[r:r + NORM_ROWS, :]
        ms = jnp.mean(x * x, axis=-1, keepdims=True)
        h_ref[r:r + NORM_ROWS, :] = (x * lax.rsqrt(ms + NORM_EPS) * g_ref[...]).astype(h_ref.dtype)


def _in_proj_kernel(n_main, x_ref, g_ref, wm_ref, wg_ref, ws_ref, o_ref, s_ref, h_ref):
    j = pl.program_id(1)
    tm = x_ref.shape[0]
    rc = min(MM_ROWS, tm)

    @pl.when(j == 0)
    def _():
        for r in range(0, tm, rc):
            _rms_block(x_ref, g_ref, h_ref, r, rc)
            h = h_ref[r:r + rc, :]
            s_ref[r:r + rc, :] = _dot(h, ws_ref[...])
            o_ref[r:r + rc, :] = _dot(h, wm_ref[...]).astype(o_ref.dtype)

    @pl.when(jnp.logical_and(j > 0, j < n_main))
    def _():
        o_ref[...] = _dot(h_ref[...], wm_ref[...]).astype(o_ref.dtype)

    @pl.when(j >= n_main)
    def _():
        o_ref[...] = _dot(h_ref[...], wg_ref[...]).astype(o_ref.dtype)


def _in_proj(x2, gain, w_in16, w_gates16, w_small16, layer):
    n, d = x2.shape
    tm = min(MM_TM, n)
    n_main = MAIN_COLS // MM_TN
    n_gate = w_gates16.shape[2] // MM_TN
    return pl.pallas_call(
        functools.partial(_in_proj_kernel, n_main),
        grid=(n // tm, n_main + n_gate),
        in_specs=[
            pl.BlockSpec((tm, d), lambda i, j: (i, 0)),
            pl.BlockSpec((1, d), lambda i, j: (0, 0)),
            pl.BlockSpec((None, d, MM_TN), lambda i, j: (layer, 0, jnp.minimum(j, n_main - 1))),
            pl.BlockSpec((None, d, MM_TN), lambda i, j: (layer, 0, jnp.maximum(j - n_main, 0))),
            pl.BlockSpec((None, d, SMALL_COLS), lambda i, j: (layer, 0, 0)),
        ],
        out_specs=[pl.BlockSpec((tm, MM_TN), lambda i, j: (i, j)),
                   pl.BlockSpec((tm, SMALL_COLS), lambda i, j: (i, 0))],
        out_shape=[jax.ShapeDtypeStruct((n, PROJ_COLS), BF16),
                   jax.ShapeDtypeStruct((n, SMALL_COLS), F32)],
        scratch_shapes=[pltpu.VMEM((tm, d), BF16)],
        compiler_params=_params("arbitrary", "arbitrary"), name="in_proj",
    )(x2, gain, w_in16, w_gates16, w_small16)


def _moba_kernel(q_ref, k_ref, v_ref, qg_ref, kg_ref, o_ref, kn_ref, vt_ref, km_ref, sel_ref):
    qi = pl.program_id(2)
    hp, nb = kn_ref.shape[0], kn_ref.shape[1]
    blk = MOBA_BLOCK
    heads = range(hp)

    def cols(h):
        return slice(h * HEAD_DIM, (h + 1) * HEAD_DIM)

    @pl.when(qi == 0)
    def _():
        tail_rows = lax.broadcasted_iota(jnp.int32, (MOBA_VROWS - HEAD_DIM, blk), 0)
        ones_tail = jnp.where(tail_rows == 0, 1.0, 0.0).astype(BF16)

        def prep(n, carry):
            r = pl.multiple_of(n * blk, blk)
            for h in heads:
                kb = k_ref[pl.ds(r, blk), cols(h)].astype(F32)
                ms = jnp.mean(kb * kb, axis=-1, keepdims=True)
                kn = kb * lax.rsqrt(ms + NORM_EPS) * kg_ref[...]
                kn_ref[h, n] = kn.astype(BF16)
                km_ref[h, pl.ds(n, 1), :] = jnp.mean(kn, axis=0, keepdims=True)
                vt_ref[h, n, 0:HEAD_DIM, :] = v_ref[pl.ds(r, blk), cols(h)].T
                vt_ref[h, n, HEAD_DIM:MOBA_VROWS, :] = ones_tail
            return carry

        lax.fori_loop(0, nb, prep, 0)

    q = [q_ref[:, cols(h)].astype(F32) for h in heads]
    qn = [t * lax.rsqrt(jnp.mean(t * t, axis=-1, keepdims=True) + NORM_EPS) * qg_ref[...] for t in q]

    def split(t):
        hi = t.astype(BF16)
        return hi, (t - hi.astype(F32)).astype(BF16)

    km = [split(km_ref[h]) for h in heads]
    qs = [split(t) for t in qn]
    gate = [_dot_nt(km[h][0], qs[h][0]) + (_dot_nt(km[h][0], qs[h][1]) + _dot_nt(km[h][1], qs[h][0]))
            for h in heads]
    rows = lax.broadcasted_iota(jnp.int32, (nb, blk), 0).astype(F32)
    past = rows < qi.astype(F32)
    gate = [jnp.where(past, g, NEG) for g in gate]
    sel = [jnp.zeros((nb, blk), F32) for _ in heads]
    for _ in range(MOBA_TOPK):
        mx = [jnp.max(g, axis=0, keepdims=True) for g in gate]
        idx = [jnp.min(jnp.where(gate[h] == mx[h], rows, float(nb)), axis=0, keepdims=True) for h in heads]
        pick = [jnp.logical_and(rows == idx[h], mx[h] > 0.5 * NEG) for h in heads]
        sel = [jnp.where(pick[h], 1.0, sel[h]) for h in heads]
        gate = [jnp.where(pick[h], NEG, gate[h]) for h in heads]
    for h in heads:
        sel_ref[h] = sel[h]

    qb = [(t * (HEAD_DIM ** -0.5 * LOG2E)).astype(BF16) for t in qn]

    s = [_dot_nt(kn_ref[h, qi], qb[h]) for h in heads]
    kidx = lax.broadcasted_iota(jnp.int32, (blk, blk), 0)
    qidx = lax.broadcasted_iota(jnp.int32, (blk, blk), 1)
    causal = kidx <= qidx
    s = [jnp.where(causal, t, NEG) for t in s]
    m0 = [jnp.max(t, axis=0, keepdims=True) for t in s]
    p = [jnp.exp2(s[h] - m0[h]) for h in heads]
    acc0 = [_dot(vt_ref[h, qi], p[h].astype(BF16)) for h in heads]

    kvb = MOBA_KV
    units = [(h, u) for h in heads for u in range(kvb)]

    def body(t, carry):
        m, acc = carry
        js = [jnp.minimum(t * kvb + u, nb - 1) for u in range(kvb)]
        dj = {(h, u): (_dot_nt(kn_ref[h, js[u]], qb[h]) - m[h]).astype(BF16) for h, u in units}
        keep = {(h, u): (sel_ref[h, pl.ds(js[u], 1), :] > 0.0) for h, u in units}
        dj = {(h, u): jnp.where(keep[h, u], dj[h, u], jnp.full_like(dj[h, u], NEG16)) for h, u in units}
        bmax = {(h, u): jnp.max(dj[h, u], axis=0, keepdims=True) for h, u in units}
        up = []
        for h in heads:
            uh = jnp.zeros_like(bmax[h, 0])
            for u in range(kvb):
                uh = jnp.maximum(uh, bmax[h, u])
            up.append(uh)
        up32 = [t.astype(F32) for t in up]
        m_new = [m[h] + up32[h] for h in heads]
        alpha = [jnp.exp2(-up32[h]) for h in heads]
        pj = {(h, u): jnp.exp2(dj[h, u] - up[h]) for h, u in units}
        pv = {(h, u): _dot(vt_ref[h, js[u]], pj[h, u]) for h, u in units}
        acc_new = []
        for h in heads:
            ah = alpha[h] * acc[h]
            for u in range(kvb):
                ah = ah + pv[h, u]
            acc_new.append(ah)
        return m_new, acc_new

    _, acc = lax.fori_loop(0, (qi + (kvb - 1)) // kvb, body, (m0, acc0))
    for h in heads:
        out = acc[h][0:HEAD_DIM, :] / acc[h][HEAD_DIM:HEAD_DIM + 1, :]
        o_ref[:, cols(h)] = out.T.astype(o_ref.dtype)


def _moba(proj, q_gain, k_gain, batch, seq):
    n = batch * seq
    nb = seq // MOBA_BLOCK
    hp = MOBA_GROUP
    gw = hp * HEAD_DIM
    groups = MOBA_HEADS // hp
    return pl.pallas_call(
        _moba_kernel, grid=(batch, groups, nb),
        in_specs=[
            pl.BlockSpec((MOBA_BLOCK, gw), lambda b, g, i: (b * nb + i, g)),
            pl.BlockSpec((seq, gw), lambda b, g, i: (b, groups + g), pipeline_mode=pl.Buffered(1)),
            pl.BlockSpec((seq, gw), lambda b, g, i: (b, 2 * groups + g), pipeline_mode=pl.Buffered(1)),
            pl.BlockSpec((1, HEAD_DIM), lambda b, g, i: (0, 0)),
            pl.BlockSpec((1, HEAD_DIM), lambda b, g, i: (0, 0)),
        ],
        out_specs=pl.BlockSpec((MOBA_BLOCK, gw), lambda b, g, i: (b * nb + i, g)),
        out_shape=jax.ShapeDtypeStruct((n, MOBA_WIDTH), BF16),
        scratch_shapes=[
            pltpu.VMEM((hp, nb, MOBA_BLOCK, HEAD_DIM), BF16),
            pltpu.VMEM((hp, nb, MOBA_VROWS, MOBA_BLOCK), BF16),
            pltpu.VMEM((hp, nb, HEAD_DIM), F32),
            pltpu.VMEM((hp, nb, MOBA_BLOCK), F32),
        ],
        compiler_params=_params("arbitrary", "arbitrary", "arbitrary"), name="moba",
    )(proj, proj, proj, q_gain, k_gain)


def _dot_split(a16, b):
    hi = b.astype(BF16)
    lo = (b - hi.astype(F32)).astype(BF16)
    return _dot(a16, hi) + _dot(a16, lo)


def _dn_kernel(qkv_ref, z_ref, sm_ref, cw_ref, al_ref, dtb_ref, og_ref, o_ref, xe_ref, st_ref):
    c = DN_TILE
    ns, _, width = qkv_ref.shape
    seqs = range(ns)

    @pl.when(pl.program_id(1) == 0)
    def _():
        xe_ref[:, 0:DN_HIST, :] = jnp.zeros((ns, DN_HIST, width), BF16)
        st_ref[...] = jnp.zeros(st_ref.shape, F32)

    ri = lax.broadcasted_iota(jnp.int32, (c, c), 0)
    ci = lax.broadcasted_iota(jnp.int32, (c, c), 1)
    tril = ri >= ci
    strict = ri > ci
    eye = jnp.where(ri == ci, 1.0, 0.0)
    lower_blocks = []
    s = 1
    while s < c:
        same_2s = ((ri ^ ci) & ~(2 * s - 1)) == 0
        lower_blocks.append(jnp.logical_and(same_2s, jnp.logical_and((ri & s) != 0, (ci & s) == 0)))
        s *= 2
    tril16 = jnp.where(tril, 1.0, 0.0).astype(BF16)
    sr = lax.broadcasted_iota(jnp.int32, (c, DN_HIST + c), 0)
    sc = lax.broadcasted_iota(jnp.int32, (c, DN_HIST + c), 1)
    shift_all = jnp.concatenate(
        [jnp.where(sc == sr + (DN_HIST - d), 1.0, 0.0).astype(BF16) for d in range(1, DN_CONV)], axis=0)


    ys = []
    for b in seqs:
        xe_ref[b, DN_HIST:DN_HIST + c, :] = qkv_ref[b]
        xe = xe_ref[b]
        y = cw_ref[DN_CONV - 1:DN_CONV, :] * qkv_ref[b].astype(F32)
        shifted = _dot(shift_all, xe)
        for d in range(1, DN_CONV):
            y = y + cw_ref[DN_CONV - 1 - d:DN_CONV - d, :] * shifted[(d - 1) * c:d * c, :]
        xe_ref[b, 0:DN_HIST, :] = xe_ref[b, c:c + DN_HIST, :]
        ys.append(y * _sigmoid(y))

    sm = [sm_ref[b] for b in seqs]
    beta_all = [_sigmoid(t) for t in sm]
    xg = [t + dtb_ref[...] for t in sm]
    softplus = [jnp.maximum(t, 0.0) + jnp.log(1.0 + jnp.exp(-jnp.abs(t))) for t in xg]
    g_all = [-jnp.exp(al_ref[...]) * t for t in softplus]
    gcum = [_dot_split(tril16, t) for t in g_all]
    gcum_t = [t.T for t in gcum]

    units = [(b, h) for b in seqs for h in range(DN_HEADS)]

    def head_cols(b, base, h):
        return ys[b][:, base + h * HEAD_DIM:base + (h + 1) * HEAD_DIM]

    q = {u: head_cols(u[0], 0, u[1]) for u in units}
    k = {u: head_cols(u[0], DN_WIDTH, u[1]) for u in units}
    v = {u: head_cols(u[0], 2 * DN_WIDTH, u[1]) for u in units}
    q = {u: t * lax.rsqrt(jnp.sum(t * t, axis=-1, keepdims=True) + NORM_EPS) * (HEAD_DIM ** -0.5)
         for u, t in q.items()}
    k = {u: t * lax.rsqrt(jnp.sum(t * t, axis=-1, keepdims=True) + NORM_EPS) for u, t in k.items()}
    beta = {(b, h): beta_all[b][:, h:h + 1] for b, h in units}
    g_col = {(b, h): gcum[b][:, DN_HEADS + h:DN_HEADS + h + 1] for b, h in units}
    g_row = {(b, h): gcum_t[b][DN_HEADS + h:DN_HEADS + h + 1, :] for b, h in units}
    g_last = {u: t[c - 1:c, :] for u, t in g_col.items()}
    decay = {u: jnp.where(tril, jnp.exp(jnp.minimum(g_col[u] - g_row[u], 0.0)), 0.0) for u in units}
    e_g = {u: jnp.exp(t) for u, t in g_col.items()}
    kb = {u: k[u] * beta[u] for u in units}
    k16 = {u: t.astype(BF16) for u, t in k.items()}
    nmat = {u: jnp.where(strict, _dot_nt(kb[u].astype(BF16), k16[u]) * decay[u], 0.0) for u in units}
    attn = {u: (_dot_nt(q[u].astype(BF16), k16[u]) * decay[u]).astype(BF16) for u in units}

    rhs = {u: jnp.concatenate([v[u] * beta[u], kb[u] * e_g[u]], axis=1).astype(BF16) for u in units}
    n16 = {u: t.astype(BF16) for u, t in nmat.items()}
    tinv = {u: eye - jnp.where(lower_blocks[0], t, 0.0) for u, t in nmat.items()}
    for msk in lower_blocks[1:]:
        t16 = {u: t.astype(BF16) for u, t in tinv.items()}
        cm = {u: jnp.where(msk, t, jnp.zeros_like(t)) for u, t in n16.items()}
        p = {u: _dot(cm[u], t16[u]).astype(BF16) for u in units}
        tinv = {u: tinv[u] - _dot(t16[u], p[u]) for u in units}
    sol = {u: _dot(tinv[u].astype(BF16), rhs[u]) for u in units}

    state = {(b, h): st_ref[b, h] for b, h in units}
    s16 = {u: t.astype(BF16) for u, t in state.items()}
    v16 = {u: (sol[u][:, :HEAD_DIM] - _dot(sol[u][:, HEAD_DIM:].astype(BF16), s16[u])).astype(BF16)
           for u in units}
    o = {u: _dot((q[u] * e_g[u]).astype(BF16), s16[u]) + _dot(attn[u], v16[u]) for u in units}
    kd = {u: (k[u] * jnp.exp(g_last[u] - g_col[u])).astype(BF16) for u in units}
    for b, h in units:
        st_ref[b, h] = state[b, h] * jnp.exp(g_last[b, h]) + _dot_tn(kd[b, h], v16[b, h])
    for b, h in units:
        ou = o[b, h]
        on = ou * lax.rsqrt(jnp.mean(ou * ou, axis=-1, keepdims=True) + NORM_EPS) * og_ref[...]
        zh = z_ref[b, :, h * HEAD_DIM:(h + 1) * HEAD_DIM].astype(F32)
        o_ref[b, :, h * HEAD_DIM:(h + 1) * HEAD_DIM] = (on * (zh * _sigmoid(zh))).astype(o_ref.dtype)


def _deltanet(proj, small, conv_w, a_log_row, dt_bias_row, out_gain, batch, seq):
    nt = seq // DN_TILE
    ns = DN_SEQS if batch % DN_SEQS == 0 else 1
    qkv_w = 3 * DN_WIDTH
    qkv_blk = (3 * MOBA_WIDTH) // qkv_w
    z_blk = (3 * MOBA_WIDTH + qkv_w) // DN_WIDTH
    proj3 = proj.reshape(batch, seq, proj.shape[1])
    small3 = small.reshape(batch, seq, SMALL_COLS)
    y_b = pl.pallas_call(
        _dn_kernel, grid=(batch // ns, nt),
        in_specs=[
            pl.BlockSpec((ns, DN_TILE, qkv_w), lambda g, t: (g, t, qkv_blk)),
            pl.BlockSpec((ns, DN_TILE, DN_WIDTH), lambda g, t: (g, t, z_blk)),
            pl.BlockSpec((ns, DN_TILE, SMALL_COLS), lambda g, t: (g, t, 0)),
            pl.BlockSpec((8, qkv_w), lambda g, t: (0, 0)),
            pl.BlockSpec((1, SMALL_COLS), lambda g, t: (0, 0)),
            pl.BlockSpec((1, SMALL_COLS), lambda g, t: (0, 0)),
            pl.BlockSpec((1, HEAD_DIM), lambda g, t: (0, 0)),
        ],
        out_specs=pl.BlockSpec((ns, DN_TILE, DN_WIDTH), lambda g, t: (g, t, 0)),
        out_shape=jax.ShapeDtypeStruct((batch, seq, DN_WIDTH), BF16),
        scratch_shapes=[
            pltpu.VMEM((ns, DN_HIST + DN_TILE, qkv_w), BF16),
            pltpu.VMEM((ns, DN_HEADS, HEAD_DIM, HEAD_DIM), F32),
        ],
        compiler_params=_params("arbitrary", "arbitrary"), name="deltanet",
    )(proj3, proj3, small3, conv_w, a_log_row, dt_bias_row, out_gain)
    return y_b.reshape(batch * seq, DN_WIDTH)


def _attn_out_kernel(ya_ref, yb_ref, wa_ref, wb_ref, ga0_ref, ga1_ref, gb0_ref, gb1_ref, wo_ref, x_ref,
                     o_ref, mg_ref):
    tm = ya_ref.shape[0]
    rc = min(MM_ROWS, tm)
    half = D_MODEL // 2
    gates = ((ga0_ref, gb0_ref), (ga1_ref, gb1_ref))

    def dots(r, c):
        rows = slice(r * rc, (r + 1) * rc)
        cols = slice(c * half, (c + 1) * half)
        return _dot(ya_ref[rows, :], wa_ref[:, cols]), _dot(yb_ref[rows, :], wb_ref[:, cols])

    def combine(r, c, ab):
        rows = slice(r * rc, (r + 1) * rc)
        ga = _sigmoid(gates[c][0][rows, :].astype(F32))
        gb = _sigmoid(gates[c][1][rows, :].astype(F32))
        mg_ref[rows, c * half:(c + 1) * half] = (ga * ab[0] + gb * ab[1]).astype(mg_ref.dtype)

    def project(r):
        rows = slice(r * rc, (r + 1) * rc)
        o_ref[rows, :] = x_ref[rows, :] + _dot(mg_ref[rows, :], wo_ref[...])

    units = [(r, c) for r in range(tm // rc) for c in range(2)]
    pending = None
    for r, c in units:
        ab = dots(r, c)
        if pending is not None:
            combine(*pending)
            if pending[1] == 1:
                project(pending[0])
        pending = (r, c, ab)
    combine(*pending)
    project(pending[0])


def _attn_out(y_a, y_b, w_a16, w_b16, proj, w_out16, layer, x2):
    n = y_a.shape[0]
    tm = min(ATTN_OUT_TM, n)
    half = D_MODEL // 2
    ga_blk = MAIN_COLS // half
    gb_blk = ga_blk + 2
    resident = dict(pipeline_mode=pl.Buffered(1))
    return pl.pallas_call(
        _attn_out_kernel, grid=(n // tm,),
        in_specs=[
            pl.BlockSpec((tm, MOBA_WIDTH), lambda i: (i, 0)),
            pl.BlockSpec((tm, DN_WIDTH), lambda i: (i, 0)),
            pl.BlockSpec((None, MOBA_WIDTH, D_MODEL), lambda i: (layer, 0, 0), **resident),
            pl.BlockSpec((None, DN_WIDTH, D_MODEL), lambda i: (layer, 0, 0), **resident),
            pl.BlockSpec((tm, half), lambda i: (i, ga_blk)),
            pl.BlockSpec((tm, half), lambda i: (i, ga_blk + 1)),
            pl.BlockSpec((tm, half), lambda i: (i, gb_blk)),
            pl.BlockSpec((tm, half), lambda i: (i, gb_blk + 1)),
            pl.BlockSpec((None, D_MODEL, D_MODEL), lambda i: (layer, 0, 0), **resident),
            pl.BlockSpec((tm, D_MODEL), lambda i: (i, 0)),
        ],
        out_specs=pl.BlockSpec((tm, D_MODEL), lambda i: (i, 0)),
        out_shape=jax.ShapeDtypeStruct((n, D_MODEL), F32),
        scratch_shapes=[pltpu.VMEM((tm, D_MODEL), BF16)],
        compiler_params=_params("arbitrary"), name="attn_out",
    )(y_a, y_b, w_a16, w_b16, proj, proj, proj, proj, w_out16, x2)


def _res_mm_kernel(a_ref, w_ref, x_ref, o_ref):
    o_ref[...] = x_ref[...] + _dot(a_ref[...], w_ref[...])


def _res_mm(a, w, layer, x2, tn):
    n, k = a.shape
    cols = w.shape[2]
    tm = min(MM_TM, n)
    return pl.pallas_call(
        _res_mm_kernel, grid=(n // tm, cols // tn),
        in_specs=[
            pl.BlockSpec((tm, k), lambda i, j: (i, 0)),
            pl.BlockSpec((None, k, tn), lambda i, j: (layer, 0, j)),
            pl.BlockSpec((tm, tn), lambda i, j: (i, j)),
        ],
        out_specs=pl.BlockSpec((tm, tn), lambda i, j: (i, j)),
        out_shape=jax.ShapeDtypeStruct((n, cols), F32),
        compiler_params=_params("arbitrary", "arbitrary"), name="res_mm",
    )(a, w, x2)


def _ffn_in_kernel(tiles_per_seq, x_ref, g_ref, wg_ref, wu_ref, cw_ref, cb_ref, o_ref,
                   h_ref, cs_ref, tail_ref):
    i = pl.program_id(0)
    j = pl.program_id(1)
    tm = x_ref.shape[0]

    seq_start = (i % tiles_per_seq) == 0

    @pl.when(seq_start)
    def _():
        cs_ref[0:8, :] = jnp.zeros((8, cs_ref.shape[1]), F32)

    @pl.when(jnp.logical_not(seq_start))
    def _():
        cs_ref[0:8, :] = tail_ref[j]

    rc = min(FFN_ROWS, tm)

    def activation(r):
        y = cb_ref[...] + cw_ref[0:1, :] * cs_ref[6 + r * rc:6 + (r + 1) * rc, :]
        for t in range(1, FFN_CONV):
            y = y + cw_ref[t:t + 1, :] * cs_ref[6 + t + r * rc:6 + t + (r + 1) * rc, :]
        return y * _sigmoid(y)

    def step(normalise):
        for r in range(tm // rc):
            if normalise:
                _rms_block(x_ref, g_ref, h_ref, r * rc, rc)
            h = h_ref[r * rc:(r + 1) * rc, :]
            cs_ref[8 + r * rc:8 + (r + 1) * rc, :] = _dot(h, wg_ref[...])
            act = activation(r)
            o_ref[r * rc:(r + 1) * rc, :] = (act * _dot(h, wu_ref[...])).astype(o_ref.dtype)
        tail_ref[j] = cs_ref[tm:tm + 8, :]

    @pl.when(j == 0)
    def _():
        step(True)

    @pl.when(j > 0)
    def _():
        step(False)


def _ffn_in(x2, gain, w_in16, layer, conv_w, conv_b, seq):
    n, d = x2.shape
    tm = min(MM_TM, seq)
    nj = D_FF // FFN_TN
    return pl.pallas_call(
        functools.partial(_ffn_in_kernel, seq // tm),
        grid=(n // tm, nj),
        in_specs=[
            pl.BlockSpec((tm, d), lambda i, j: (i, 0)),
            pl.BlockSpec((1, d), lambda i, j: (0, 0)),
            pl.BlockSpec((None, d, FFN_TN), lambda i, j: (layer, 0, j)),
            pl.BlockSpec((None, d, FFN_TN), lambda i, j: (layer, 0, nj + j)),
            pl.BlockSpec((8, FFN_TN), lambda i, j: (0, j)),
            pl.BlockSpec((1, FFN_TN), lambda i, j: (0, j)),
        ],
        out_specs=pl.BlockSpec((tm, FFN_TN), lambda i, j: (i, j)),
        out_shape=jax.ShapeDtypeStruct((n, D_FF), BF16),
        scratch_shapes=[
            pltpu.VMEM((tm, d), BF16),
            pltpu.VMEM((tm + 8, FFN_TN), F32),
            pltpu.VMEM((nj, 8, FFN_TN), F32),
        ],
        compiler_params=_params("arbitrary", "arbitrary"), name="ffn_in",
    )(x2, gain, w_in16, w_in16, conv_w, conv_b)


def _ffn(x2, gain, w_ffn_in16, conv_w, conv_b, w_down16, layer, seq):
    act = _ffn_in(x2, gain[None, :], w_ffn_in16, layer, _pad_rows(conv_w, 8), conv_b[None, :], seq)
    return _res_mm(act, w_down16, layer, x2, FFN_TN)


def _pad_rows(w, rows):
    return jnp.pad(w, ((0, rows - w.shape[0]), (0, 0)))


def _lane_row(v, offset):
    return jnp.zeros((1, SMALL_COLS), F32).at[0, offset:offset + v.shape[0]].set(v.astype(F32))


def kernel(x, attn_norm, w_in, moba_q_norm, moba_k_norm, dn_conv, dn_a_log, dn_dt_bias, dn_out_norm,
           w_branch_a, w_branch_b, w_out, ffn_norm, w_ffn_in, ffn_conv, ffn_conv_bias, w_ffn_down):
    batch, seq, d = x.shape
    depth = w_in.shape[0]
    n = batch * seq
    assert d == D_MODEL and seq % MOBA_BLOCK == 0 and seq % DN_TILE == 0 and n % min(MM_TM, n) == 0
    small_w = 2 * DN_HEADS
    x2 = x.reshape(n, d)
    w_in16 = w_in.astype(BF16)
    w_gates16 = w_in16[:, :, MAIN_COLS + small_w:]
    w_small16 = jnp.pad(w_in16[:, :, MAIN_COLS:MAIN_COLS + small_w], ((0, 0), (0, 0), (0, SMALL_COLS - small_w)))
    w_a16 = w_branch_a.astype(BF16)
    w_b16 = w_branch_b.astype(BF16)
    w_out16 = w_out.astype(BF16)
    w_ffn_in16 = w_ffn_in.astype(BF16)
    w_ffn_down16 = w_ffn_down.astype(BF16)
    for l in range(depth):
        proj, small = _in_proj(x2, attn_norm[l][None, :], w_in16, w_gates16, w_small16, l)
        y_a = _moba(proj, moba_q_norm[l][None, :], moba_k_norm[l][None, :], batch, seq)
        y_b = _deltanet(proj, small, _pad_rows(dn_conv[l], 8), _lane_row(dn_a_log[l], DN_HEADS),
                        _lane_row(dn_dt_bias[l], DN_HEADS), dn_out_norm[l][None, :], batch, seq)
        x2 = _attn_out(y_a, y_b, w_a16, w_b16, proj, w_out16, l, x2)
        x2 = _ffn(x2, ffn_norm[l], w_ffn_in16, ffn_conv[l], ffn_conv_bias[l], w_ffn_down16, l, seq)
    return x2.reshape(batch, seq, d)
```

```python
import functools

import jax
import jax.numpy as jnp
from jax import lax
from jax.experimental import pallas as pl
from jax.experimental.pallas import tpu as pltpu

F32 = jnp.float32
BF16 = jnp.bfloat16

D_MODEL = 2048
HEAD_DIM = 128
MOBA_HEADS = 8
MOBA_WIDTH = MOBA_HEADS * HEAD_DIM
MOBA_BLOCK = 256
MOBA_TOPK = 3
DN_HEADS = 8
DN_WIDTH = DN_HEADS * HEAD_DIM
DN_CONV = 4
D_FF = 5632
FFN_CONV = 3
NORM_EPS = 1e-6

MAIN_COLS = 3 * MOBA_WIDTH + 4 * DN_WIDTH
PROJ_COLS = MAIN_COLS + 2 * D_MODEL
SMALL_COLS = 128

VMEM_LIMIT_BYTES = 56 * 1024 * 1024

MM_TM = 1024
MM_TN = 1024
PREP_ROWS = 256
NORM_ROWS = 128
MM_ROWS = 256
ATTN_OUT_TM = 512
MOBA_GROUP = 8
MOBA_KV = 2
MOBA_VROWS = HEAD_DIM + 16
LOG2E = 1.4426950408889634
DN_TILE = 128
DN_SEQS = 2
DN_HIST = 16
FFN_TN = 512
FFN_ROWS = 512
NEG = -1e30
NEG16 = -1e30


def _sigmoid(x):
    return 1.0 / (1.0 + jnp.exp(-x))


def _dot(a, b):
    return jnp.dot(a, b, preferred_element_type=F32)


def _dot_nt(a, b, precision=None):
    return lax.dot_general(a, b, (((1,), (1,)), ((), ())), precision=precision,
                           preferred_element_type=F32)


def _dot_tn(a, b):
    return lax.dot_general(a, b, (((0,), (0,)), ((), ())), preferred_element_type=F32)


def _params(*sem):
    return pltpu.CompilerParams(dimension_semantics=sem, vmem_limit_bytes=VMEM_LIMIT_BYTES)


def _w_in_prep_kernel(n_main, small_w, a_ref, b_ref, o_ref, s_ref):
    j = pl.program_id(1)
    rows = a_ref.shape[0]
    width = a_ref.shape[1]

    @pl.when(j < n_main)
    def _():
        for r in range(0, rows, PREP_ROWS):
            o_ref[r:r + PREP_ROWS, :] = a_ref[r:r + PREP_ROWS, :].astype(o_ref.dtype)

    @pl.when(j >= n_main)
    def _():
        lane = lax.broadcasted_iota(jnp.int32, (PREP_ROWS, width), 1)
        for r in range(0, rows, PREP_ROWS):
            a = pltpu.roll(a_ref[r:r + PREP_ROWS, :], width - small_w, 1)
            b = pltpu.roll(b_ref[r:r + PREP_ROWS, :], width - small_w, 1)
            o_ref[r:r + PREP_ROWS, :] = jnp.where(lane < width - small_w, a, b).astype(o_ref.dtype)

    @pl.when(j == n_main)
    def _():
        lane = lax.broadcasted_iota(jnp.int32, (rows, SMALL_COLS), 1)
        s_ref[...] = jnp.where(lane < small_w, a_ref[:, 0:SMALL_COLS], 0.0).astype(s_ref.dtype)


def _w_in_prep(w_in, small_w):
    depth, d, cols = w_in.shape
    n_main = MAIN_COLS // MM_TN
    n_tiles = PROJ_COLS // MM_TN
    last_blk = (cols - 1) // MM_TN
    return pl.pallas_call(
        functools.partial(_w_in_prep_kernel, n_main, small_w),
        grid=(depth, n_tiles),
        in_specs=[
            pl.BlockSpec((None, d, MM_TN), lambda l, j: (l, 0, j)),
            pl.BlockSpec((None, d, MM_TN), lambda l, j: (l, 0, jnp.clip(j + 1, n_main, last_blk))),
        ],
        out_specs=[pl.BlockSpec((None, d, MM_TN), lambda l, j: (l, 0, j)),
                   pl.BlockSpec((None, d, SMALL_COLS), lambda l, j: (l, 0, 0))],
        out_shape=[jax.ShapeDtypeStruct((depth, d, PROJ_COLS), BF16),
                   jax.ShapeDtypeStruct((depth, d, SMALL_COLS), BF16)],
        compiler_params=_params("arbitrary", "arbitrary"), name="w_in_prep",
    )(w_in, w_in)


def _rms_block(x_ref, g_ref, h_ref, row0, rows):
    for r in range(row0, row0 + rows, NORM_ROWS):
        x = x_ref[r:r + NORM_ROWS, :]
        ms = jnp.mean(x * x, axis=-1, keepdims=True)
        h_ref[r:r + NORM_ROWS, :] = (x * lax.rsqrt(ms + NORM_EPS) * g_ref[...]).astype(h_ref.dtype)


def _in_proj_kernel(x_ref, g_ref, w_ref, ws_ref, o_ref, s_ref, h_ref):
    j = pl.program_id(1)
    tm = x_ref.shape[0]
    rc = min(MM_ROWS, tm)

    @pl.when(j == 0)
    def _():
        for r in range(0, tm, rc):
            _rms_block(x_ref, g_ref, h_ref, r, rc)
            h = h_ref[r:r + rc, :]
            s_ref[r:r + rc, :] = _dot(h, ws_ref[...])
            o_ref[r:r + rc, :] = _dot(h, w_ref[...]).astype(o_ref.dtype)

    @pl.when(j > 0)
    def _():
        o_ref[...] = _dot(h_ref[...], w_ref[...]).astype(o_ref.dtype)


def _in_proj(x2, gain, w_proj16, w_small16, layer):
    n, d = x2.shape
    tm = min(MM_TM, n)
    return pl.pallas_call(
        _in_proj_kernel,
        grid=(n // tm, PROJ_COLS // MM_TN),
        in_specs=[
            pl.BlockSpec((tm, d), lambda i, j: (i, 0)),
            pl.BlockSpec((1, d), lambda i, j: (0, 0)),
            pl.BlockSpec((None, d, MM_TN), lambda i, j: (layer, 0, j)),
            pl.BlockSpec((None, d, SMALL_COLS), lambda i, j: (layer, 0, 0)),
        ],
        out_specs=[pl.BlockSpec((tm, MM_TN), lambda i, j: (i, j)),
                   pl.BlockSpec((tm, SMALL_COLS), lambda i, j: (i, 0))],
        out_shape=[jax.ShapeDtypeStruct((n, PROJ_COLS), BF16),
                   jax.ShapeDtypeStruct((n, SMALL_COLS), F32)],
        scratch_shapes=[pltpu.VMEM((tm, d), BF16)],
        compiler_params=_params("arbitrary", "arbitrary"), name="in_proj",
    )(x2, gain, w_proj16, w_small16)


def _moba_kernel(q_ref, k_ref, v_ref, qg_ref, kg_ref, o_ref, kn_ref, vt_ref, km_ref, sel_ref):
    qi = pl.program_id(2)
    hp, nb = kn_ref.shape[0], kn_ref.shape[1]
    blk = MOBA_BLOCK
    heads = range(hp)

    def cols(h):
        return slice(h * HEAD_DIM, (h + 1) * HEAD_DIM)

    @pl.when(qi == 0)
    def _():
        tail_rows = lax.broadcasted_iota(jnp.int32, (MOBA_VROWS - HEAD_DIM, blk), 0)
        ones_tail = jnp.where(tail_rows == 0, 1.0, 0.0).astype(BF16)

        def prep(n, carry):
            r = pl.multiple_of(n * blk, blk)
            for h in heads:
                kb = k_ref[pl.ds(r, blk), cols(h)].astype(F32)
                ms = jnp.mean(kb * kb, axis=-1, keepdims=True)
                kn = kb * lax.rsqrt(ms + NORM_EPS) * kg_ref[...]
                kn_ref[h, n] = kn.astype(BF16)
                km_ref[h, pl.ds(n, 1), :] = jnp.mean(kn, axis=0, keepdims=True)
                vt_ref[h, n, 0:HEAD_DIM, :] = v_ref[pl.ds(r, blk), cols(h)].T
                vt_ref[h, n, HEAD_DIM:MOBA_VROWS, :] = ones_tail
            return carry

        lax.fori_loop(0, nb, prep, 0)

    q = [q_ref[:, cols(h)].astype(F32) for h in heads]
    qn = [t * lax.rsqrt(jnp.mean(t * t, axis=-1, keepdims=True) + NORM_EPS) * qg_ref[...] for t in q]

    def split(t):
        hi = t.astype(BF16)
        return hi, (t - hi.astype(F32)).astype(BF16)

    km = [split(km_ref[h]) for h in heads]
    qs = [split(t) for t in qn]
    gate = [_dot_nt(km[h][0], qs[h][0]) + (_dot_nt(km[h][0], qs[h][1]) + _dot_nt(km[h][1], qs[h][0]))
            for h in heads]
    rows = lax.broadcasted_iota(jnp.int32, (nb, blk), 0).astype(F32)
    past = rows < qi.astype(F32)
    gate = [jnp.where(past, g, NEG) for g in gate]
    sel = [jnp.zeros((nb, blk), F32) for _ in heads]
    for _ in range(MOBA_TOPK):
        mx = [jnp.max(g, axis=0, keepdims=True) for g in gate]
        idx = [jnp.min(jnp.where(gate[h] == mx[h], rows, float(nb)), axis=0, keepdims=True) for h in heads]
        pick = [jnp.logical_and(rows == idx[h], mx[h] > 0.5 * NEG) for h in heads]
        sel = [jnp.where(pick[h], 1.0, sel[h]) for h in heads]
        gate = [jnp.where(pick[h], NEG, gate[h]) for h in heads]
    for h in heads:
        sel_ref[h] = sel[h]

    qb = [(t * (HEAD_DIM ** -0.5 * LOG2E)).astype(BF16) for t in qn]

    s = [_dot_nt(kn_ref[h, qi], qb[h]) for h in heads]
    kidx = lax.broadcasted_iota(jnp.int32, (blk, blk), 0)
    qidx = lax.broadcasted_iota(jnp.int32, (blk, blk), 1)
    causal = kidx <= qidx
    s = [jnp.where(causal, t, NEG) for t in s]
    m0 = [jnp.max(t, axis=0, keepdims=True) for t in s]
    p = [jnp.exp2(s[h] - m0[h]) for h in heads]
    acc0 = [_dot(vt_ref[h, qi], p[h].astype(BF16)) for h in heads]

    kvb = MOBA_KV
    units = [(h, u) for h in heads for u in range(kvb)]

    def body(t, carry):
        m, acc = carry
        js = [jnp.minimum(t * kvb + u, nb - 1) for u in range(kvb)]
        dj = {(h, u): (_dot_nt(kn_ref[h, js[u]], qb[h]) - m[h]).astype(BF16) for h, u in units}
        keep = {(h, u): (sel_ref[h, pl.ds(js[u], 1), :] > 0.0) for h, u in units}
        dj = {(h, u): jnp.where(keep[h, u], dj[h, u], jnp.full_like(dj[h, u], NEG16)) for h, u in units}
        bmax = {(h, u): jnp.max(dj[h, u], axis=0, keepdims=True) for h, u in units}
        up = []
        for h in heads:
            uh = jnp.zeros_like(bmax[h, 0])
            for u in range(kvb):
                uh = jnp.maximum(uh, bmax[h, u])
            up.append(uh)
        up32 = [t.astype(F32) for t in up]
        m_new = [m[h] + up32[h] for h in heads]
        alpha = [jnp.exp2(-up32[h]) for h in heads]
        pj = {(h, u): jnp.exp2(dj[h, u] - up[h]) for h, u in units}
        pv = {(h, u): _dot(vt_ref[h, js[u]], pj[h, u]) for h, u in units}
        acc_new = []
        for h in heads:
            ah = alpha[h] * acc[h]
            for u in range(kvb):
                ah = ah + pv[h, u]
            acc_new.append(ah)
        return m_new, acc_new

    _, acc = lax.fori_loop(0, (qi + (kvb - 1)) // kvb, body, (m0, acc0))
    for h in heads:
        out = acc[h][0:HEAD_DIM, :] / acc[h][HEAD_DIM:HEAD_DIM + 1, :]
        o_ref[:, cols(h)] = out.T.astype(o_ref.dtype)


def _moba(proj, q_gain, k_gain, batch, seq):
    n = batch * seq
    nb = seq // MOBA_BLOCK
    hp = MOBA_GROUP
    gw = hp * HEAD_DIM
    groups = MOBA_HEADS // hp
    return pl.pallas_call(
        _moba_kernel, grid=(batch, groups, nb),
        in_specs=[
            pl.BlockSpec((MOBA_BLOCK, gw), lambda b, g, i: (b * nb + i, g)),
            pl.BlockSpec((seq, gw), lambda b, g, i: (b, groups + g), pipeline_mode=pl.Buffered(1)),
            pl.BlockSpec((seq, gw), lambda b, g, i: (b, 2 * groups + g), pipeline_mode=pl.Buffered(1)),
            pl.BlockSpec((1, HEAD_DIM), lambda b, g, i: (0, 0)),
            pl.BlockSpec((1, HEAD_DIM), lambda b, g, i: (0, 0)),
        ],
        out_specs=pl.BlockSpec((MOBA_BLOCK, gw), lambda b, g, i: (b * nb + i, g)),
        out_shape=jax.ShapeDtypeStruct((n, MOBA_WIDTH), BF16),
        scratch_shapes=[
            pltpu.VMEM((hp, nb, MOBA_BLOCK, HEAD_DIM), BF16),
            pltpu.VMEM((hp, nb, MOBA_VROWS, MOBA_BLOCK), BF16),
            pltpu.VMEM((hp, nb, HEAD_DIM), F32),
            pltpu.VMEM((hp, nb, MOBA_BLOCK), F32),
        ],
        compiler_params=_params("arbitrary", "arbitrary", "arbitrary"), name="moba",
    )(proj, proj, proj, q_gain, k_gain)


def _dot_split(a16, b):
    hi = b.astype(BF16)
    lo = (b - hi.astype(F32)).astype(BF16)
    return _dot(a16, hi) + _dot(a16, lo)


def _dn_kernel(qkv_ref, z_ref, sm_ref, cw_ref, al_ref, dtb_ref, og_ref, o_ref, xe_ref, st_ref):
    c = DN_TILE
    ns, _, width = qkv_ref.shape
    seqs = range(ns)

    @pl.when(pl.program_id(1) == 0)
    def _():
        xe_ref[:, 0:DN_HIST, :] = jnp.zeros((ns, DN_HIST, width), BF16)
        st_ref[...] = jnp.zeros(st_ref.shape, F32)

    ri = lax.broadcasted_iota(jnp.int32, (c, c), 0)
    ci = lax.broadcasted_iota(jnp.int32, (c, c), 1)
    tril = ri >= ci
    strict = ri > ci
    eye = jnp.where(ri == ci, 1.0, 0.0)
    lower_blocks = []
    s = 1
    while s < c:
        same_2s = ((ri ^ ci) & ~(2 * s - 1)) == 0
        lower_blocks.append(jnp.logical_and(same_2s, jnp.logical_and((ri & s) != 0, (ci & s) == 0)))
        s *= 2
    tril16 = jnp.where(tril, 1.0, 0.0).astype(BF16)
    sr = lax.broadcasted_iota(jnp.int32, (c, DN_HIST + c), 0)
    sc = lax.broadcasted_iota(jnp.int32, (c, DN_HIST + c), 1)
    shift_all = jnp.concatenate(
        [jnp.where(sc == sr + (DN_HIST - d), 1.0, 0.0).astype(BF16) for d in range(1, DN_CONV)], axis=0)


    ys = []
    for b in seqs:
        xe_ref[b, DN_HIST:DN_HIST + c, :] = qkv_ref[b]
        xe = xe_ref[b]
        y = cw_ref[DN_CONV - 1:DN_CONV, :] * qkv_ref[b].astype(F32)
        shifted = _dot(shift_all, xe)
        for d in range(1, DN_CONV):
            y = y + cw_ref[DN_CONV - 1 - d:DN_CONV - d, :] * shifted[(d - 1) * c:d * c, :]
        xe_ref[b, 0:DN_HIST, :] = xe_ref[b, c:c + DN_HIST, :]
        ys.append(y * _sigmoid(y))

    sm = [sm_ref[b] for b in seqs]
    beta_all = [_sigmoid(t) for t in sm]
    xg = [t + dtb_ref[...] for t in sm]
    softplus = [jnp.maximum(t, 0.0) + jnp.log(1.0 + jnp.exp(-jnp.abs(t))) for t in xg]
    g_all = [-jnp.exp(al_ref[...]) * t for t in softplus]
    gcum = [_dot_split(tril16, t) for t in g_all]
    gcum_t = [t.T for t in gcum]

    units = [(b, h) for b in seqs for h in range(DN_HEADS)]

    def head_cols(b, base, h):
        return ys[b][:, base + h * HEAD_DIM:base + (h + 1) * HEAD_DIM]

    q = {u: head_cols(u[0], 0, u[1]) for u in units}
    k = {u: head_cols(u[0], DN_WIDTH, u[1]) for u in units}
    v = {u: head_cols(u[0], 2 * DN_WIDTH, u[1]) for u in units}
    q = {u: t * lax.rsqrt(jnp.sum(t * t, axis=-1, keepdims=True) + NORM_EPS) * (HEAD_DIM ** -0.5)
         for u, t in q.items()}
    k = {u: t * lax.rsqrt(jnp.sum(t * t, axis=-1, keepdims=True) + NORM_EPS) for u, t in k.items()}
    beta = {(b, h): beta_all[b][:, h:h + 1] for b, h in units}
    g_col = {(b, h): gcum[b][:, DN_HEADS + h:DN_HEADS + h + 1] for b, h in units}
    g_row = {(b, h): gcum_t[b][DN_HEADS + h:DN_HEADS + h + 1, :] for b, h in units}
    g_last = {u: t[c - 1:c, :] for u, t in g_col.items()}
    decay = {u: jnp.where(tril, jnp.exp(jnp.minimum(g_col[u] - g_row[u], 0.0)), 0.0) for u in units}
    e_g = {u: jnp.exp(t) for u, t in g_col.items()}
    kb = {u: k[u] * beta[u] for u in units}
    k16 = {u: t.astype(BF16) for u, t in k.items()}
    nmat = {u: jnp.where(strict, _dot_nt(kb[u].astype(BF16), k16[u]) * decay[u], 0.0) for u in units}
    attn = {u: (_dot_nt(q[u].astype(BF16), k16[u]) * decay[u]).astype(BF16) for u in units}

    rhs = {u: jnp.concatenate([v[u] * beta[u], kb[u] * e_g[u]], axis=1).astype(BF16) for u in units}
    n16 = {u: t.astype(BF16) for u, t in nmat.items()}
    tinv = {u: eye - jnp.where(lower_blocks[0], t, 0.0) for u, t in nmat.items()}
    for msk in lower_blocks[1:]:
        t16 = {u: t.astype(BF16) for u, t in tinv.items()}
        cm = {u: jnp.where(msk, t, jnp.zeros_like(t)) for u, t in n16.items()}
        p = {u: _dot(cm[u], t16[u]).astype(BF16) for u in units}
        tinv = {u: tinv[u] - _dot(t16[u], p[u]) for u in units}
    sol = {u: _dot(tinv[u].astype(BF16), rhs[u]) for u in units}

    state = {(b, h): st_ref[b, h] for b, h in units}
    s16 = {u: t.astype(BF16) for u, t in state.items()}
    v16 = {u: (sol[u][:, :HEAD_DIM] - _dot(sol[u][:, HEAD_DIM:].astype(BF16), s16[u])).astype(BF16)
           for u in units}
    o = {u: _dot((q[u] * e_g[u]).astype(BF16), s16[u]) + _dot(attn[u], v16[u]) for u in units}
    kd = {u: (k[u] * jnp.exp(g_last[u] - g_col[u])).astype(BF16) for u in units}
    for b, h in units:
        st_ref[b, h] = state[b, h] * jnp.exp(g_last[b, h]) + _dot_tn(kd[b, h], v16[b, h])
    for b, h in units:
        ou = o[b, h]
        on = ou * lax.rsqrt(jnp.mean(ou * ou, axis=-1, keepdims=True) + NORM_EPS) * og_ref[...]
        zh = z_ref[b, :, h * HEAD_DIM:(h + 1) * HEAD_DIM].astype(F32)
        o_ref[b, :, h * HEAD_DIM:(h + 1) * HEAD_DIM] = (on * (zh * _sigmoid(zh))).astype(o_ref.dtype)


def _deltanet(proj, small, conv_w, a_log_row, dt_bias_row, out_gain, batch, seq):
    nt = seq // DN_TILE
    ns = DN_SEQS if batch % DN_SEQS == 0 else 1
    qkv_w = 3 * DN_WIDTH
    qkv_blk = (3 * MOBA_WIDTH) // qkv_w
    z_blk = (3 * MOBA_WIDTH + qkv_w) // DN_WIDTH
    proj3 = proj.reshape(batch, seq, proj.shape[1])
    small3 = small.reshape(batch, seq, SMALL_COLS)
    y_b = pl.pallas_call(
        _dn_kernel, grid=(batch // ns, nt),
        in_specs=[
            pl.BlockSpec((ns, DN_TILE, qkv_w), lambda g, t: (g, t, qkv_blk)),
            pl.BlockSpec((ns, DN_TILE, DN_WIDTH), lambda g, t: (g, t, z_blk)),
            pl.BlockSpec((ns, DN_TILE, SMALL_COLS), lambda g, t: (g, t, 0)),
            pl.BlockSpec((8, qkv_w), lambda g, t: (0, 0)),
            pl.BlockSpec((1, SMALL_COLS), lambda g, t: (0, 0)),
            pl.BlockSpec((1, SMALL_COLS), lambda g, t: (0, 0)),
            pl.BlockSpec((1, HEAD_DIM), lambda g, t: (0, 0)),
        ],
        out_specs=pl.BlockSpec((ns, DN_TILE, DN_WIDTH), lambda g, t: (g, t, 0)),
        out_shape=jax.ShapeDtypeStruct((batch, seq, DN_WIDTH), BF16),
        scratch_shapes=[
            pltpu.VMEM((ns, DN_HIST + DN_TILE, qkv_w), BF16),
            pltpu.VMEM((ns, DN_HEADS, HEAD_DIM, HEAD_DIM), F32),
        ],
        compiler_params=_params("arbitrary", "arbitrary"), name="deltanet",
    )(proj3, proj3, small3, conv_w, a_log_row, dt_bias_row, out_gain)
    return y_b.reshape(batch * seq, DN_WIDTH)


def _attn_out_kernel(ya_ref, yb_ref, wa_ref, wb_ref, ga0_ref, ga1_ref, gb0_ref, gb1_ref, wo_ref, x_ref,
                     o_ref, mg_ref):
    tm = ya_ref.shape[0]
    rc = min(MM_ROWS, tm)
    half = D_MODEL // 2
    gates = ((ga0_ref, gb0_ref), (ga1_ref, gb1_ref))

    def dots(r, c):
        rows = slice(r * rc, (r + 1) * rc)
        cols = slice(c * half, (c + 1) * half)
        return _dot(ya_ref[rows, :], wa_ref[:, cols]), _dot(yb_ref[rows, :], wb_ref[:, cols])

    def combine(r, c, ab):
        rows = slice(r * rc, (r + 1) * rc)
        ga = _sigmoid(gates[c][0][rows, :].astype(F32))
        gb = _sigmoid(gates[c][1][rows, :].astype(F32))
        mg_ref[rows, c * half:(c + 1) * half] = (ga * ab[0] + gb * ab[1]).astype(mg_ref.dtype)

    def project(r):
        rows = slice(r * rc, (r + 1) * rc)
        o_ref[rows, :] = x_ref[rows, :] + _dot(mg_ref[rows, :], wo_ref[...])

    units = [(r, c) for r in range(tm // rc) for c in range(2)]
    pending = None
    for r, c in units:
        ab = dots(r, c)
        if pending is not None:
            combine(*pending)
            if pending[1] == 1:
                project(pending[0])
        pending = (r, c, ab)
    combine(*pending)
    project(pending[0])


def _attn_out(y_a, y_b, w_a16, w_b16, proj, w_out16, layer, x2):
    n = y_a.shape[0]
    tm = min(ATTN_OUT_TM, n)
    half = D_MODEL // 2
    ga_blk = MAIN_COLS // half
    gb_blk = ga_blk + 2
    resident = dict(pipeline_mode=pl.Buffered(1))
    return pl.pallas_call(
        _attn_out_kernel, grid=(n // tm,),
        in_specs=[
            pl.BlockSpec((tm, MOBA_WIDTH), lambda i: (i, 0)),
            pl.BlockSpec((tm, DN_WIDTH), lambda i: (i, 0)),
            pl.BlockSpec((None, MOBA_WIDTH, D_MODEL), lambda i: (layer, 0, 0), **resident),
            pl.BlockSpec((None, DN_WIDTH, D_MODEL), lambda i: (layer, 0, 0), **resident),
            pl.BlockSpec((tm, half), lambda i: (i, ga_blk)),
            pl.BlockSpec((tm, half), lambda i: (i, ga_blk + 1)),
            pl.BlockSpec((tm, half), lambda i: (i, gb_blk)),
            pl.BlockSpec((tm, half), lambda i: (i, gb_blk + 1)),
            pl.BlockSpec((None, D_MODEL, D_MODEL), lambda i: (layer, 0, 0), **resident),
            pl.BlockSpec((tm, D_MODEL), lambda i: (i, 0)),
        ],
        out_specs=pl.BlockSpec((tm, D_MODEL), lambda i: (i, 0)),
        out_shape=jax.ShapeDtypeStruct((n, D_MODEL), F32),
        scratch_shapes=[pltpu.VMEM((tm, D_MODEL), BF16)],
        compiler_params=_params("arbitrary"), name="attn_out",
    )(y_a, y_b, w_a16, w_b16, proj, proj, proj, proj, w_out16, x2)


def _res_mm_kernel(a_ref, w_ref, x_ref, o_ref):
    o_ref[...] = x_ref[...] + _dot(a_ref[...], w_ref[...])


def _res_mm(a, w, layer, x2, tn):
    n, k = a.shape
    cols = w.shape[2]
    tm = min(MM_TM, n)
    return pl.pallas_call(
        _res_mm_kernel, grid=(n // tm, cols // tn),
        in_specs=[
            pl.BlockSpec((tm, k), lambda i, j: (i, 0)),
            pl.BlockSpec((None, k, tn), lambda i, j: (layer, 0, j)),
            pl.BlockSpec((tm, tn), lambda i, j: (i, j)),
        ],
        out_specs=pl.BlockSpec((tm, tn), lambda i, j: (i, j)),
        out_shape=jax.ShapeDtypeStruct((n, cols), F32),
        compiler_params=_params("arbitrary", "arbitrary"), name="res_mm",
    )(a, w, x2)


def _ffn_in_kernel(tiles_per_seq, x_ref, g_ref, wg_ref, wu_ref, cw_ref, cb_ref, o_ref,
                   h_ref, cs_ref, tail_ref):
    i = pl.program_id(0)
    j = pl.program_id(1)
    tm = x_ref.shape[0]

    seq_start = (i % tiles_per_seq) == 0

    @pl.when(seq_start)
    def _():
        cs_ref[0:8, :] = jnp.zeros((8, cs_ref.shape[1]), F32)

    @pl.when(jnp.logical_not(seq_start))
    def _():
        cs_ref[0:8, :] = tail_ref[j]

    rc = min(FFN_ROWS, tm)

    def activation(r):
        y = cb_ref[...] + cw_ref[0:1, :] * cs_ref[6 + r * rc:6 + (r + 1) * rc, :]
        for t in range(1, FFN_CONV):
            y = y + cw_ref[t:t + 1, :] * cs_ref[6 + t + r * rc:6 + t + (r + 1) * rc, :]
        return y * _sigmoid(y)

    def step(normalise):
        for r in range(tm // rc):
            if normalise:
                _rms_block(x_ref, g_ref, h_ref, r * rc, rc)
            h = h_ref[r * rc:(r + 1) * rc, :]
            cs_ref[8 + r * rc:8 + (r + 1) * rc, :] = _dot(h, wg_ref[...])
            act = activation(r)
            o_ref[r * rc:(r + 1) * rc, :] = (act * _dot(h, wu_ref[...])).astype(o_ref.dtype)
        tail_ref[j] = cs_ref[tm:tm + 8, :]

    @pl.when(j == 0)
    def _():
        step(True)

    @pl.when(j > 0)
    def _():
        step(False)


def _ffn_in(x2, gain, w_in16, layer, conv_w, conv_b, seq):
    n, d = x2.shape
    tm = min(MM_TM, seq)
    nj = D_FF // FFN_TN
    return pl.pallas_call(
        functools.partial(_ffn_in_kernel, seq // tm),
        grid=(n // tm, nj),
        in_specs=[
            pl.BlockSpec((tm, d), lambda i, j: (i, 0)),
            pl.BlockSpec((1, d), lambda i, j: (0, 0)),
            pl.BlockSpec((None, d, FFN_TN), lambda i, j: (layer, 0, j)),
            pl.BlockSpec((None, d, FFN_TN), lambda i, j: (layer, 0, nj + j)),
            pl.BlockSpec((8, FFN_TN), lambda i, j: (0, j)),
            pl.BlockSpec((1, FFN_TN), lambda i, j: (0, j)),
        ],
        out_specs=pl.BlockSpec((tm, FFN_TN), lambda i, j: (i, j)),
        out_shape=jax.ShapeDtypeStruct((n, D_FF), BF16),
        scratch_shapes=[
            pltpu.VMEM((tm, d), BF16),
            pltpu.VMEM((tm + 8, FFN_TN), F32),
            pltpu.VMEM((nj, 8, FFN_TN), F32),
        ],
        compiler_params=_params("arbitrary", "arbitrary"), name="ffn_in",
    )(x2, gain, w_in16, w_in16, conv_w, conv_b)


def _ffn(x2, gain, w_ffn_in16, conv_w, conv_b, w_down16, layer, seq):
    act = _ffn_in(x2, gain[None, :], w_ffn_in16, layer, _pad_rows(conv_w, 8), conv_b[None, :], seq)
    return _res_mm(act, w_down16, layer, x2, FFN_TN)


def _pad_rows(w, rows):
    return jnp.pad(w, ((0, rows - w.shape[0]), (0, 0)))


def _lane_row(v, offset):
    return jnp.zeros((1, SMALL_COLS), F32).at[0, offset:offset + v.shape[0]].set(v.astype(F32))


def kernel(x, attn_norm, w_in, moba_q_norm, moba_k_norm, dn_conv, dn_a_log, dn_dt_bias, dn_out_norm,
           w_branch_a, w_branch_b, w_out, ffn_norm, w_ffn_in, ffn_conv, ffn_conv_bias, w_ffn_down):
    batch, seq, d = x.shape
    depth = w_in.shape[0]
    n = batch * seq
    assert d == D_MODEL and seq % MOBA_BLOCK == 0 and seq % DN_TILE == 0 and n % min(MM_TM, n) == 0
    small_w = 2 * DN_HEADS
    x2 = x.reshape(n, d)
    w_proj16, w_small16 = _w_in_prep(w_in, small_w)
    w_a16 = w_branch_a.astype(BF16)
    w_b16 = w_branch_b.astype(BF16)
    w_out16 = w_out.astype(BF16)
    w_ffn_in16 = w_ffn_in.astype(BF16)
    w_ffn_down16 = w_ffn_down.astype(BF16)
    for l in range(depth):
        proj, small = _in_proj(x2, attn_norm[l][None, :], w_proj16, w_small16, l)
        y_a = _moba(proj, moba_q_norm[l][None, :], moba_k_norm[l][None, :], batch, seq)
        y_b = _deltanet(proj, small, _pad_rows(dn_conv[l], 8), _lane_row(dn_a_log[l], DN_HEADS),
                        _lane_row(dn_dt_bias[l], DN_HEADS), dn_out_norm[l][None, :], batch, seq)
        x2 = _attn_out(y_a, y_b, w_a16, w_b16, proj, w_out16, l, x2)
        x2 = _ffn(x2, ffn_norm[l], w_ffn_in16, ffn_conv[l], ffn_conv_bias[l], w_ffn_down16, l, seq)
    return x2.reshape(batch, seq, d)
```
